```python
import math
import jax, jax.numpy as jnp
from jax import lax
import numpy as np

D_MODEL = 2048
BATCH = 16
SEQ = 2048
DEPTH = 4

N_MIXERS = 4
NORM_EPS = 1e-6
ROPE_THETA = 10000.0
NEG_INF = -1e30

CC_CH = D_MODEL
CC_WIDTH = 31

DSA_HEAD_DIM = 128
DSA_HEADS = 8
DSA_CONFIGS = ((128, 1), (512, 4), (2048, 16))
DSA_GROUPS = len(DSA_CONFIGS)
DSA_BLOCK = 128

MLA_HEADS = 16
MLA_Q_RANK = 512
MLA_KV_RANK = 512
MLA_NOPE = 128
MLA_ROPE = 64
MLA_V = 128
ATTN_BLOCK = 128

SG_CHUNK = 128
SG_HALF = 3 * D_MODEL
SG_GROUPS = 8

FFN_DIM = 5632
FFN_WIDTH = 3

kernel_name = 'hybrid_interleaved_conv_dilated_mla_sgu_convffn'


def rmsnorm(x, g):
    x32 = x.astype(jnp.float32)
    y = x32 * lax.rsqrt(jnp.mean(x32 * x32, axis=-1, keepdims=True) + NORM_EPS)
    return (y * g.astype(jnp.float32)).astype(x.dtype)


def layernorm(x, g, b):
    x32 = x.astype(jnp.float32)
    mu = jnp.mean(x32, axis=-1, keepdims=True)
    xc = x32 - mu
    y = xc * lax.rsqrt(jnp.mean(xc * xc, axis=-1, keepdims=True) + NORM_EPS)
    return (y * g.astype(jnp.float32) + b.astype(jnp.float32)).astype(x.dtype)


def rope_tables(seq, dim):
    pos = jnp.arange(seq, dtype=jnp.float32)
    inv = ROPE_THETA ** (-(jnp.arange(0, dim, 2, dtype=jnp.float32) / dim))
    ang = pos[:, None] * inv[None, :]
    return jnp.cos(ang)[:, None, :], jnp.sin(ang)[:, None, :]


def apply_rope(x, cos, sin):
    x1, x2 = jnp.split(x, 2, axis=-1)
    c = cos.astype(x.dtype)
    s = sin.astype(x.dtype)
    return jnp.concatenate([x1 * c - x2 * s, x2 * c + x1 * s], axis=-1)


def causal_dwconv(x, w, b):
    k = w.shape[0]
    y = lax.conv_general_dilated(
        x, w[:, None, :].astype(x.dtype), window_strides=(1,), padding=((k - 1, 0),),
        dimension_numbers=('NWC', 'WIO', 'NWC'), feature_group_count=x.shape[-1])
    return y + b


def conformer_conv_module(h, w_in, b_in, dw_w, dw_b, ln_g, ln_b, w_out, b_out):
    z = h @ w_in + b_in
    a, g = jnp.split(z, 2, axis=-1)
    z = a * jax.nn.sigmoid(g)
    z = causal_dwconv(z, dw_w, dw_b)
    z = jax.nn.silu(layernorm(z, ln_g, ln_b))
    return z @ w_out + b_out


def dilated_group_attention(q, k, v, window, dil):
    B, S, H, Dh = q.shape
    span = window // dil
    L = S // dil
    nb = -(-L // DSA_BLOCK)
    Lp = nb * DSA_BLOCK

    def to_sub(t):
        t = t.reshape(B, L, dil, H, Dh).transpose(0, 2, 3, 1, 4)
        t = jnp.pad(t, ((0, 0), (0, 0), (0, 0), (0, Lp - L), (0, 0)))
        return t.reshape(B, dil, H, nb, DSA_BLOCK, Dh)

    def with_prev(t):
        prev = jnp.pad(t[:, :, :, :-1], ((0, 0), (0, 0), (0, 0), (1, 0), (0, 0), (0, 0)))
        return jnp.concatenate([prev, t], axis=4)

    qb = to_sub(q)
    kw = with_prev(to_sub(k))
    vw = with_prev(to_sub(v))
    s = jnp.einsum('bdhnqc,bdhnkc->bdhnqk', qb, kw).astype(jnp.float32) * (Dh ** -0.5)
    qi = jnp.arange(DSA_BLOCK)[:, None]
    kj = jnp.arange(2 * DSA_BLOCK)[None, :]
    dist = DSA_BLOCK + qi - kj
    band = (dist >= 0) & (dist <= span)
    blk = jnp.arange(nb)[:, None, None]
    mask = band[None] & ((blk > 0) | (kj[None] >= DSA_BLOCK))
    s = jnp.where(mask, s, NEG_INF)
    m = jnp.max(s, axis=-1, keepdims=True)
    p = jnp.exp(s - m)
    l = jnp.sum(p, axis=-1, keepdims=True)
    o = jnp.einsum('bdhnqk,bdhnkc->bdhnqc', p.astype(v.dtype), vw).astype(jnp.float32) / l
    lse = (m + jnp.log(l))[..., 0]
    o = o.reshape(B, dil, H, Lp, Dh)[:, :, :, :L].transpose(0, 3, 1, 2, 4).reshape(B, S, H, Dh)
    lse = lse.reshape(B, dil, H, Lp)[:, :, :, :L].transpose(0, 3, 1, 2).reshape(B, S, H)
    return o, lse


def dilated_attention_mixer(h, w_qkv, w_o):
    B, S, _ = h.shape
    qkv = (h @ w_qkv).reshape(B, S, DSA_GROUPS, 3, DSA_HEADS, DSA_HEAD_DIM)
    cos, sin = rope_tables(S, DSA_HEAD_DIM)
    outs, lses = [], []
    for g, (window, dil) in enumerate(DSA_CONFIGS):
        q = apply_rope(qkv[:, :, g, 0], cos, sin)
        k = apply_rope(qkv[:, :, g, 1], cos, sin)
        o, lse = dilated_group_attention(q, k, qkv[:, :, g, 2], window, dil)
        outs.append(o)
        lses.append(lse)
    alpha = jax.nn.softmax(jnp.stack(lses, axis=0), axis=0)
    o = jnp.einsum('gbsh,gbshc->bshc', alpha, jnp.stack(outs, axis=0))
    return o.reshape(B, S, DSA_HEADS * DSA_HEAD_DIM).astype(h.dtype) @ w_o


def blocked_causal_attention(q, k, v, scale):
    B, S, H, Dq = q.shape
    Dv = v.shape[-1]
    nb = S // ATTN_BLOCK
    qb = q.reshape(B, nb, ATTN_BLOCK, H, Dq).transpose(1, 0, 2, 3, 4)
    kpos = jnp.arange(S)

    def one_block(args):
        qi, i = args
        s = jnp.einsum('bqhc,bkhc->bhqk', qi, k).astype(jnp.float32) * scale
        qpos = i * ATTN_BLOCK + jnp.arange(ATTN_BLOCK)
        s = jnp.where((kpos[None, :] <= qpos[:, None])[None, None], s, NEG_INF)
        p = jax.nn.softmax(s, axis=-1)
        return jnp.einsum('bhqk,bkhc->bqhc', p.astype(v.dtype), v)

    out = lax.map(one_block, (qb, jnp.arange(nb)))
    return out.transpose(1, 0, 2, 3, 4).reshape(B, S, H, Dv)


def mla_mixer(h, w_in, q_norm, w_qb, kv_norm, w_kvb, w_o):
    B, S, _ = h.shape
    c = h @ w_in
    cq = c[..., :MLA_Q_RANK]
    ckv = c[..., MLA_Q_RANK:MLA_Q_RANK + MLA_KV_RANK]
    k_pe = c[..., MLA_Q_RANK + MLA_KV_RANK:]
    cos, sin = rope_tables(S, MLA_ROPE)
    q = (rmsnorm(cq, q_norm) @ w_qb).reshape(B, S, MLA_HEADS, MLA_NOPE + MLA_ROPE)
    q = jnp.concatenate([q[..., :MLA_NOPE], apply_rope(q[..., MLA_NOPE:], cos, sin)], axis=-1)
    kv = (rmsnorm(ckv, kv_norm) @ w_kvb).reshape(B, S, MLA_HEADS, MLA_NOPE + MLA_V)
    k_pe = apply_rope(k_pe[:, :, None, :], cos, sin)
    k = jnp.concatenate([kv[..., :MLA_NOPE], jnp.broadcast_to(k_pe, (B, S, MLA_HEADS, MLA_ROPE))], axis=-1)
    v = kv[..., MLA_NOPE:]
    o = blocked_causal_attention(q, k, v, (MLA_NOPE + MLA_ROPE) ** -0.5)
    return o.reshape(B, S, MLA_HEADS * MLA_V) @ w_o


def chunked_sgu_mixer(h, w_in, b_in, ln_g, ln_b, w_s, b_s, w_out, b_out):
    B, S, _ = h.shape
    nc = S // SG_CHUNK
    z = jax.nn.gelu(h @ w_in + b_in, approximate=False)
    u, v = jnp.split(z, 2, axis=-1)
    v = layernorm(v, ln_g, ln_b).reshape(B, nc, SG_CHUNK, SG_GROUPS, SG_HALF // SG_GROUPS)
    ws = w_s * jnp.tril(jnp.ones((SG_CHUNK, SG_CHUNK), w_s.dtype))[None]
    v = jnp.einsum('gts,bnsgc->bntgc', ws, v) + b_s.T[None, None, :, :, None]
    return (u * v.reshape(B, S, SG_HALF)) @ w_out + b_out


def conv_ffn(h, w_up, dw_w, dw_b, w_down):
    z = causal_dwconv(h @ w_up, dw_w, dw_b)
    g, a = jnp.split(z, 2, axis=-1)
    return (jax.nn.silu(g) * a) @ w_down


def setup_inputs(seed: int = 0) -> dict:
    key = jax.random.key(seed)
    counter = [0]

    def nk():
        counter[0] += 1
        return jax.random.fold_in(key, counter[0])

    def w(shape, fan_in):
        return jax.random.normal(nk(), shape, jnp.float32) * (fan_in ** -0.5)

    def gain(n):
        return 1.0 + 0.05 * jax.random.normal(nk(), (n,), jnp.float32)

    def bias(shape):
        return 0.02 * jax.random.normal(nk(), shape, jnp.float32)

    def ffn(p, pre):
        p[pre + '_norm_ffn'] = gain(D_MODEL)
        p[pre + '_ffn_w_up'] = w((D_MODEL, 2 * FFN_DIM), D_MODEL)
        p[pre + '_ffn_dw_w'] = w((FFN_WIDTH, 2 * FFN_DIM), FFN_WIDTH)
        p[pre + '_ffn_dw_b'] = bias((2 * FFN_DIM,))
        p[pre + '_ffn_w_down'] = w((FFN_DIM, D_MODEL), FFN_DIM)

    p = {}
    p['x'] = jax.random.normal(nk(), (BATCH, SEQ, D_MODEL), jnp.float32)
    p['l0_norm_mix'] = gain(D_MODEL)
    p['l0_cc_w_in'] = w((D_MODEL, 2 * CC_CH), D_MODEL)
    p['l0_cc_b_in'] = bias((2 * CC_CH,))
    p['l0_cc_dw_w'] = w((CC_WIDTH, CC_CH), CC_WIDTH)
    p['l0_cc_dw_b'] = bias((CC_CH,))
    p['l0_cc_ln_g'] = gain(CC_CH)
    p['l0_cc_ln_b'] = bias((CC_CH,))
    p['l0_cc_w_out'] = w((CC_CH, D_MODEL), CC_CH)
    p['l0_cc_b_out'] = bias((D_MODEL,))
    ffn(p, 'l0')
    p['l1_norm_mix'] = gain(D_MODEL)
    p['l1_dsa_w_qkv'] = w((D_MODEL, DSA_GROUPS * 3 * DSA_HEADS * DSA_HEAD_DIM), D_MODEL)
    p['l1_dsa_w_o'] = w((DSA_HEADS * DSA_HEAD_DIM, D_MODEL), DSA_HEADS * DSA_HEAD_DIM)
    ffn(p, 'l1')
    p['l2_norm_mix'] = gain(D_MODEL)
    p['l2_mla_w_in'] = w((D_MODEL, MLA_Q_RANK + MLA_KV_RANK + MLA_ROPE), D_MODEL)
    p['l2_mla_q_norm'] = gain(MLA_Q_RANK)
    p['l2_mla_w_qb'] = w((MLA_Q_RANK, MLA_HEADS * (MLA_NOPE + MLA_ROPE)), MLA_Q_RANK)
    p['l2_mla_kv_norm'] = gain(MLA_KV_RANK)
    p['l2_mla_w_kvb'] = w((MLA_KV_RANK, MLA_HEADS * (MLA_NOPE + MLA_V)), MLA_KV_RANK)
    p['l2_mla_w_o'] = w((MLA_HEADS * MLA_V, D_MODEL), MLA_HEADS * MLA_V)
    ffn(p, 'l2')
    p['l3_norm_mix'] = gain(D_MODEL)
    p['l3_sg_w_in'] = w((D_MODEL, 2 * SG_HALF), D_MODEL)
    p['l3_sg_b_in'] = bias((2 * SG_HALF,))
    p['l3_sg_ln_g'] = gain(SG_HALF)
    p['l3_sg_ln_b'] = bias((SG_HALF,))
    p['l3_sg_w_s'] = w((SG_GROUPS, SG_CHUNK, SG_CHUNK), SG_CHUNK)
    p['l3_sg_b_s'] = 1.0 + 0.05 * jax.random.normal(nk(), (SG_GROUPS, SG_CHUNK), jnp.float32)
    p['l3_sg_w_out'] = w((SG_HALF, D_MODEL), SG_HALF)
    p['l3_sg_b_out'] = bias((D_MODEL,))
    ffn(p, 'l3')
    p['final_norm'] = gain(D_MODEL)
    return p


def reference(x,
              l0_norm_mix, l0_cc_w_in, l0_cc_b_in, l0_cc_dw_w, l0_cc_dw_b, l0_cc_ln_g, l0_cc_ln_b,
              l0_cc_w_out, l0_cc_b_out,
              l0_norm_ffn, l0_ffn_w_up, l0_ffn_dw_w, l0_ffn_dw_b, l0_ffn_w_down,
              l1_norm_mix, l1_dsa_w_qkv, l1_dsa_w_o,
              l1_norm_ffn, l1_ffn_w_up, l1_ffn_dw_w, l1_ffn_dw_b, l1_ffn_w_down,
              l2_norm_mix, l2_mla_w_in, l2_mla_q_norm, l2_mla_w_qb, l2_mla_kv_norm, l2_mla_w_kvb,
              l2_mla_w_o,
              l2_norm_ffn, l2_ffn_w_up, l2_ffn_dw_w, l2_ffn_dw_b, l2_ffn_w_down,
              l3_norm_mix, l3_sg_w_in, l3_sg_b_in, l3_sg_ln_g, l3_sg_ln_b, l3_sg_w_s, l3_sg_b_s,
              l3_sg_w_out, l3_sg_b_out,
              l3_norm_ffn, l3_ffn_w_up, l3_ffn_dw_w, l3_ffn_dw_b, l3_ffn_w_down,
              final_norm):
    mixers = (
        lambda h: conformer_conv_module(h, l0_cc_w_in, l0_cc_b_in, l0_cc_dw_w, l0_cc_dw_b,
                                        l0_cc_ln_g, l0_cc_ln_b, l0_cc_w_out, l0_cc_b_out),
        lambda h: dilated_attention_mixer(h, l1_dsa_w_qkv, l1_dsa_w_o),
        lambda h: mla_mixer(h, l2_mla_w_in, l2_mla_q_norm, l2_mla_w_qb, l2_mla_kv_norm,
                            l2_mla_w_kvb, l2_mla_w_o),
        lambda h: chunked_sgu_mixer(h, l3_sg_w_in, l3_sg_b_in, l3_sg_ln_g, l3_sg_ln_b,
                                    l3_sg_w_s, l3_sg_b_s, l3_sg_w_out, l3_sg_b_out),
    )
    norms_mix = (l0_norm_mix, l1_norm_mix, l2_norm_mix, l3_norm_mix)
    norms_ffn = (l0_norm_ffn, l1_norm_ffn, l2_norm_ffn, l3_norm_ffn)
    ffns = (
        (l0_ffn_w_up, l0_ffn_dw_w, l0_ffn_dw_b, l0_ffn_w_down),
        (l1_ffn_w_up, l1_ffn_dw_w, l1_ffn_dw_b, l1_ffn_w_down),
        (l2_ffn_w_up, l2_ffn_dw_w, l2_ffn_dw_b, l2_ffn_w_down),
        (l3_ffn_w_up, l3_ffn_dw_w, l3_ffn_dw_b, l3_ffn_w_down),
    )
    h = x
    for i in range(DEPTH):
        h = h + mixers[i % N_MIXERS](rmsnorm(h, norms_mix[i]))
        h = h + conv_ffn(rmsnorm(h, norms_ffn[i]), *ffns[i])
    return rmsnorm(h, final_norm)
```

```python
import functools
import math

import jax
import jax.numpy as jnp
from jax import lax
from jax.experimental import pallas as pl
from jax.experimental.pallas import tpu as pltpu

F32 = jnp.float32
BF16 = jnp.bfloat16

NORM_EPS = 1e-6
ROPE_THETA = 10000.0
NEG_INF = -1e30

DSA_HEAD_DIM = 128
DSA_CONFIGS = ((128, 1), (512, 4), (2048, 16))
DSA_BLOCK = 128
MLA_HEADS = 16
MLA_NOPE = 128
MLA_ROPE = 64
MLA_V = 128
SG_CHUNK = 128
SG_GROUPS = 8

V7X_LANES = 128
V7X_SUBLANES = 8
V7X_VMEM_LIMIT_BYTES = 56 * 1024 * 1024

MM_BM, MM_BN = 1024, 1024
MM_PROLOGUE_BM = 512
FFN_BM, FFN_FC = 512, 512
MLA_BQ, MLA_BK = 512, 512
SGU_BM = 512
CONV_CB, CONV_ROWS = 256, 64


def _params(*semantics):
    return pltpu.CompilerParams(dimension_semantics=semantics,
                                vmem_limit_bytes=V7X_VMEM_LIMIT_BYTES)


def _sigmoid(x):
    return 1.0 / (1.0 + jnp.exp(-x))


def _tile(n, want):
    t = min(n, want)
    while n % t:
        t -= 1
    return t


def _rmsnorm_kernel(x_ref, g_ref, o_ref):
    x = x_ref[...].astype(F32)
    y = x * lax.rsqrt(jnp.mean(x * x, axis=-1, keepdims=True) + NORM_EPS)
    o_ref[...] = (y * g_ref[...]).astype(o_ref.dtype)


def rmsnorm(x, g, out_dtype):
    m, d = x.shape
    bm = _tile(m, 512)
    return pl.pallas_call(
        _rmsnorm_kernel,
        out_shape=jax.ShapeDtypeStruct((m, d), out_dtype),
        grid=(m // bm,),
        in_specs=[pl.BlockSpec((bm, d), lambda i: (i, 0)),
                  pl.BlockSpec((1, d), lambda i: (0, 0))],
        out_specs=pl.BlockSpec((bm, d), lambda i: (i, 0)),
        compiler_params=_params("parallel"),
        name="rmsnorm",
    )(x, g.reshape(1, d).astype(F32))


def _mm_kernel(*refs, prologue, has_bias, glu, gelu, rope_mask, residual, stats):
    it = iter(refs)
    x_ref = next(it)
    pg_ref = next(it) if prologue else None
    pb_ref = next(it) if prologue == "ln_silu" else None
    w_ref = next(it)
    w2_ref = next(it) if glu else None
    b_ref = next(it) if has_bias else None
    b2_ref = next(it) if (glu and has_bias) else None
    cos_ref = next(it) if rope_mask else None
    sin_ref = next(it) if rope_mask else None
    res_ref = next(it) if residual else None
    o_ref = next(it)
    st_ref = next(it) if stats else None
    xs_ref = next(it) if prologue else None

    j = pl.program_id(1)

    if prologue:
        @pl.when(j == 0)
        def _():
            xf = x_ref[...].astype(F32)
            if prologue == "rms":
                y = xf * lax.rsqrt(jnp.mean(xf * xf, axis=-1, keepdims=True) + NORM_EPS)
                y = y * pg_ref[...]
            else:
                mu = jnp.mean(xf, axis=-1, keepdims=True)
                xc = xf - mu
                y = xc * lax.rsqrt(jnp.mean(xc * xc, axis=-1, keepdims=True) + NORM_EPS)
                y = y * pg_ref[...] + pb_ref[...]
                y = y * _sigmoid(y)
            xs_ref[...] = y.astype(BF16)
        xv = xs_ref[...]
    else:
        xv = x_ref[...]

    acc = jnp.dot(xv, w_ref[...], preferred_element_type=F32)
    if has_bias:
        acc = acc + b_ref[...]
    if glu:
        gate = jnp.dot(xv, w2_ref[...], preferred_element_type=F32)
        if has_bias:
            gate = gate + b2_ref[...]
        acc = acc * _sigmoid(gate)
    if gelu:
        acc = 0.5 * acc * (1.0 + lax.erf(acc * (2.0 ** -0.5)))
    if stats:
        s1 = jnp.sum(acc, axis=-1, keepdims=True)
        s2 = jnp.sum(acc * acc, axis=-1, keepdims=True)
        lane = lax.broadcasted_iota(jnp.int32, st_ref.shape, 1)
        upd = jnp.where(lane == 0, s1, jnp.where(lane == 1, s2, 0.0))

        @pl.when(j == 0)
        def _():
            st_ref[...] = upd

        @pl.when(j > 0)
        def _():
            st_ref[...] += upd
    if residual:
        acc = acc + res_ref[...]
    if rope_mask:
        cos = cos_ref[...]
        sin = sin_ref[...]
        for k, on in enumerate(rope_mask):
            sl = acc[:, k * V7X_LANES:(k + 1) * V7X_LANES]
            if on:
                sl = sl * cos + pltpu.roll(sl, V7X_LANES // 2, 1) * sin
            o_ref[:, k * V7X_LANES:(k + 1) * V7X_LANES] = sl.astype(o_ref.dtype)
    else:
        o_ref[...] = acc.astype(o_ref.dtype)


def matmul(x, w, *, x_cols=None, prologue=None, pg=None, pb=None, w2=None, bias=None,
           bias2=None, gelu=False, rope=None, rope_mask=None, seq=None, residual=None,
           stats=False, out_dtype=BF16, bn=None):
    m = x.shape[0]
    k, n = w.shape
    kblk = 0 if x_cols is None else x_cols[0]
    bm = _tile(m, MM_PROLOGUE_BM if prologue else MM_BM)
    bn = _tile(n, bn or MM_BN)
    glu = w2 is not None
    has_bias = bias is not None
    if rope_mask:
        assert bn % V7X_LANES == 0 and len(rope_mask) == bn // V7X_LANES
        assert seq % bm == 0
    args, specs = [x], [pl.BlockSpec((bm, k), lambda i, j: (i, kblk))]
    if prologue:
        args.append(pg.reshape(1, k).astype(F32))
        specs.append(pl.BlockSpec((1, k), lambda i, j: (0, 0)))
        if prologue == "ln_silu":
            args.append(pb.reshape(1, k).astype(F32))
            specs.append(pl.BlockSpec((1, k), lambda i, j: (0, 0)))
    args.append(w)
    specs.append(pl.BlockSpec((k, bn), lambda i, j: (0, j)))
    if glu:
        args.append(w2)
        specs.append(pl.BlockSpec((k, bn), lambda i, j: (0, j)))
    if has_bias:
        args.append(bias.reshape(1, n).astype(F32))
        specs.append(pl.BlockSpec((1, bn), lambda i, j: (0, j)))
        if glu:
            args.append(bias2.reshape(1, n).astype(F32))
            specs.append(pl.BlockSpec((1, bn), lambda i, j: (0, j)))
    if rope_mask:
        tiles_per_seq = seq // bm
        for t in rope:
            args.append(t)
            specs.append(pl.BlockSpec((bm, V7X_LANES), lambda i, j: (i % tiles_per_seq, 0)))
    if residual is not None:
        args.append(residual)
        specs.append(pl.BlockSpec((bm, bn), lambda i, j: (i, j)))
    out_shape = [jax.ShapeDtypeStruct((m, n), out_dtype)]
    out_specs = [pl.BlockSpec((bm, bn), lambda i, j: (i, j))]
    if stats:
        out_shape.append(jax.ShapeDtypeStruct((m, V7X_LANES), F32))
        out_specs.append(pl.BlockSpec((bm, V7X_LANES), lambda i, j: (i, 0)))
    scratch = [pltpu.VMEM((bm, k), BF16)] if prologue else []
    kern = functools.partial(
        _mm_kernel, prologue=prologue, has_bias=has_bias, glu=glu, gelu=gelu,
        rope_mask=tuple(rope_mask) if rope_mask else None,
        residual=residual is not None, stats=stats)
    out = pl.pallas_call(
        kern,
        out_shape=out_shape,
        grid=(m // bm, n // bn),
        in_specs=specs,
        out_specs=out_specs,
        scratch_shapes=scratch,
        compiler_params=_params("parallel", "arbitrary"),
        name="matmul",
    )(*args)
    return out if stats else out[0]


def _dwconv_kernel(x_ref, w_ref, b_ref, o_ref, sh_ref, *, width, row_chunk):
    s, cb = x_ref.shape[1], x_ref.shape[2]
    pad = sh_ref.shape[1] - s
    x = x_ref[0].astype(F32)
    xz = jnp.concatenate([x, jnp.zeros((V7X_SUBLANES, cb), F32)], axis=0)
    zero_top = jnp.zeros((pad, cb), F32)
    for r in range(V7X_SUBLANES):
        sh_ref[r, :pad, :] = zero_top
        shifted = xz if r == 0 else pltpu.roll(xz, r, 0)
        sh_ref[r, pad:, :] = shifted[:s]

    w = w_ref[...]
    bias = b_ref[...]

    def chunk(c, carry):
        base = pl.multiple_of(c * row_chunk, row_chunk)
        acc = jnp.zeros((row_chunk, cb), F32) + bias
        for kk in range(width):
            shift = width - 1 - kk
            q, r = divmod(shift, V7X_SUBLANES)
            start = pl.multiple_of(base + (pad - q * V7X_SUBLANES), V7X_SUBLANES)
            tap = sh_ref[r, pl.ds(start, row_chunk), :]
            acc = acc + tap * w[kk:kk + 1, :]
        o_ref[0, pl.ds(base, row_chunk), :] = acc.astype(o_ref.dtype)
        return carry

    lax.fori_loop(0, s // row_chunk, chunk, 0)


def causal_dwconv(x, w, b):
    bsz, s, c = x.shape
    width = w.shape[0]
    cb = _tile(c, CONV_CB)
    row_chunk = _tile(s, CONV_ROWS)
    pad = -(-(width - 1) // V7X_SUBLANES) * V7X_SUBLANES
    return pl.pallas_call(
        functools.partial(_dwconv_kernel, width=width, row_chunk=row_chunk),
        out_shape=jax.ShapeDtypeStruct((bsz, s, c), BF16),
        grid=(bsz, c // cb),
        in_specs=[pl.BlockSpec((1, s, cb), lambda i, j: (i, 0, j)),
                  pl.BlockSpec((width, cb), lambda i, j: (0, j)),
                  pl.BlockSpec((1, cb), lambda i, j: (0, j))],
        out_specs=pl.BlockSpec((1, s, cb), lambda i, j: (i, 0, j)),
        scratch_shapes=[pltpu.VMEM((V7X_SUBLANES, pad + s, cb), F32)],
        compiler_params=_params("parallel", "parallel"),
        name="dwconv",
    )(x, w.astype(F32), b.reshape(1, c).astype(F32))


def _ffn_kernel(x_ref, wg_ref, wa_ref, cwg_ref, cwa_ref, cbg_ref, cba_ref, wd_ref, h_ref,
                o_ref, acc_ref, zs_ref, carry_g_ref, carry_a_ref, *, tiles_per_seq):
    i = pl.program_id(0)
    j = pl.program_id(1)
    nj = pl.num_programs(1)
    bm = x_ref.shape[0]
    seq_start = (i % tiles_per_seq) == 0
    x = x_ref[...]

    def conv(w_ref, cw_ref, cb_ref, carry_ref):
        z = jnp.dot(x, w_ref[...], preferred_element_type=F32)

        @pl.when(seq_start)
        def _():
            zs_ref[:V7X_SUBLANES, :] = jnp.zeros((V7X_SUBLANES, z.shape[1]), F32)

        @pl.when(jnp.logical_not(seq_start))
        def _():
            zs_ref[:V7X_SUBLANES, :] = carry_ref[j]

        carry_ref[j] = z[bm - V7X_SUBLANES:, :]
        zs_ref[V7X_SUBLANES:, :] = z
        zp = zs_ref[...]
        z1 = pltpu.roll(zp, 1, 0)[V7X_SUBLANES:]
        z2 = pltpu.roll(zp, 2, 0)[V7X_SUBLANES:]
        cw = cw_ref[...]
        return z2 * cw[0:1, :] + z1 * cw[1:2, :] + z * cw[2:3, :] + cb_ref[...]

    g = conv(wg_ref, cwg_ref, cbg_ref, carry_g_ref)
    a = conv(wa_ref, cwa_ref, cba_ref, carry_a_ref)
    act = (g * _sigmoid(g) * a).astype(BF16)
    part = jnp.dot(act, wd_ref[...], preferred_element_type=F32)

    @pl.when(j == 0)
    def _():
        acc_ref[...] = part

    @pl.when(j > 0)
    def _():
        acc_ref[...] += part

    @pl.when(j == nj - 1)
    def _():
        o_ref[...] = acc_ref[...] + h_ref[...]


def conv_ffn(xn, h, w_up, dw_w, dw_b, w_down, *, seq):
    m, d = xn.shape
    f = w_down.shape[0]
    bm = _tile(seq, FFN_BM)
    fc = _tile(f, FFN_FC)
    nf = f // fc
    wu = w_up.astype(BF16)
    wd = w_down.astype(BF16)
    cw = dw_w.astype(F32)
    cb = dw_b.reshape(1, 2 * f).astype(F32)
    kern = functools.partial(_ffn_kernel, tiles_per_seq=seq // bm)
    return pl.pallas_call(
        kern,
        out_shape=jax.ShapeDtypeStruct((m, d), F32),
        grid=(m // bm, nf),
        in_specs=[
            pl.BlockSpec((bm, d), lambda i, j: (i, 0)),
            pl.BlockSpec((d, fc), lambda i, j: (0, j)),
            pl.BlockSpec((d, fc), lambda i, j: (0, j + nf)),
            pl.BlockSpec((3, fc), lambda i, j: (0, j)),
            pl.BlockSpec((3, fc), lambda i, j: (0, j + nf)),
            pl.BlockSpec((1, fc), lambda i, j: (0, j)),
            pl.BlockSpec((1, fc), lambda i, j: (0, j + nf)),
            pl.BlockSpec((fc, d), lambda i, j: (j, 0)),
            pl.BlockSpec((bm, d), lambda i, j: (i, 0)),
        ],
        out_specs=pl.BlockSpec((bm, d), lambda i, j: (i, 0)),
        scratch_shapes=[
            pltpu.VMEM((bm, d), F32),
            pltpu.VMEM((bm + V7X_SUBLANES, fc), F32),
            pltpu.VMEM((nf, V7X_SUBLANES, fc), F32),
            pltpu.VMEM((nf, V7X_SUBLANES, fc), F32),
        ],
        compiler_params=_params("arbitrary", "arbitrary"),
        name="conv_ffn",
    )(xn, wu, wu, cw, cw, cb, cb, wd, h)


def _band_attn_kernel(q_ref, kp_ref, kc_ref, vp_ref, vc_ref, o_ref, lse_ref, *, heads, span):
    n = pl.program_id(2)
    blk = q_ref.shape[1]
    qi = lax.broadcasted_iota(jnp.int32, (blk, 2 * blk), 0)
    kj = lax.broadcasted_iota(jnp.int32, (blk, 2 * blk), 1)
    dist = blk + qi - kj
    first_key = jnp.where(n > 0, 0, blk)
    mask = (dist >= 0) & (dist <= span) & (kj >= first_key)
    lane = lax.broadcasted_iota(jnp.int32, (blk, V7X_LANES), 1)
    lse_tile = jnp.zeros((blk, V7X_LANES), F32)
    dh = DSA_HEAD_DIM
    for hh in range(heads):
        cols = slice(hh * dh, (hh + 1) * dh)
        q = q_ref[0, :, cols]
        k = jnp.concatenate([kp_ref[0, :, cols], kc_ref[0, :, cols]], axis=0)
        v = jnp.concatenate([vp_ref[0, :, cols], vc_ref[0, :, cols]], axis=0)
        s = lax.dot_general(q, k, (((1,), (1,)), ((), ())), preferred_element_type=F32)
        s = jnp.where(mask, s, NEG_INF)
        mx = jnp.max(s, axis=-1, keepdims=True)
        p = jnp.exp(s - mx)
        l = jnp.sum(p, axis=-1, keepdims=True)
        o = jnp.dot(p.astype(BF16), v, preferred_element_type=F32) / l
        o_ref[0, :, cols] = o.astype(o_ref.dtype)
        lse_tile = jnp.where(lane == hh, mx + jnp.log(l), lse_tile)
    lse_ref[0] = lse_tile


def band_attention(qk, v, group, window, dil, *, bsz, seq, heads):
    hw = heads * DSA_HEAD_DIM
    n_groups = v.shape[1] // hw
    ln = seq // dil
    blk = DSA_BLOCK
    nb = ln // blk
    span = window // dil
    qk3 = qk.reshape(bsz, ln, dil * qk.shape[1])
    v3 = v.reshape(bsz, ln, dil * v.shape[1])
    qcol = lambda r: r * 2 * n_groups + 2 * group
    vcol = lambda r: r * n_groups + group
    prev = lambda n: jnp.maximum(n - 1, 0)
    o, lse = pl.pallas_call(
        functools.partial(_band_attn_kernel, heads=heads, span=span),
        out_shape=[jax.ShapeDtypeStruct((bsz, ln, dil * hw), BF16),
                   jax.ShapeDtypeStruct((bsz, ln, dil * V7X_LANES), F32)],
        grid=(bsz, dil, nb),
        in_specs=[
            pl.BlockSpec((1, blk, hw), lambda b, r, n: (b, n, qcol(r))),
            pl.BlockSpec((1, blk, hw), lambda b, r, n: (b, prev(n), qcol(r) + 1)),
            pl.BlockSpec((1, blk, hw), lambda b, r, n: (b, n, qcol(r) + 1)),
            pl.BlockSpec((1, blk, hw), lambda b, r, n: (b, prev(n), vcol(r))),
            pl.BlockSpec((1, blk, hw), lambda b, r, n: (b, n, vcol(r))),
        ],
        out_specs=[pl.BlockSpec((1, blk, hw), lambda b, r, n: (b, n, r)),
                   pl.BlockSpec((1, blk, V7X_LANES), lambda b, r, n: (b, n, r))],
        compiler_params=_params("parallel", "parallel", "parallel"),
        name="band_attn",
    )(qk3, qk3, qk3, v3, v3)
    return o.reshape(bsz * seq, hw), lse.reshape(bsz * seq, V7X_LANES)


def _merge_kernel(*refs, groups, heads):
    o_refs = refs[:groups]
    lse_refs = refs[groups:2 * groups]
    out_ref = refs[2 * groups]
    lses = [r[...] for r in lse_refs]
    mx = functools.reduce(jnp.maximum, lses)
    es = [jnp.exp(x - mx) for x in lses]
    inv = 1.0 / functools.reduce(lambda a, b: a + b, es)
    alphas = [e * inv for e in es]
    dh = DSA_HEAD_DIM
    for hh in range(heads):
        cols = slice(hh * dh, (hh + 1) * dh)
        acc = None
        for g in range(groups):
            t = alphas[g][:, hh:hh + 1] * o_refs[g][:, cols].astype(F32)
            acc = t if acc is None else acc + t
        out_ref[:, cols] = acc.astype(out_ref.dtype)


def merge_groups(outs, lses, *, heads):
    m, hw = outs[0].shape
    groups = len(outs)
    bm = _tile(m, 1024)
    return pl.pallas_call(
        functools.partial(_merge_kernel, groups=groups, heads=heads),
        out_shape=jax.ShapeDtypeStruct((m, hw), BF16),
        grid=(m // bm,),
        in_specs=[pl.BlockSpec((bm, hw), lambda i: (i, 0))] * groups
        + [pl.BlockSpec((bm, V7X_LANES), lambda i: (i, 0))] * groups,
        out_specs=pl.BlockSpec((bm, hw), lambda i: (i, 0)),
        compiler_params=_params("parallel"),
        name="merge_groups",
    )(*outs, *lses)


def _mla_attn_kernel(q_ref, kv_ref, kpe_ref, o_ref, *, bk):
    qi = pl.program_id(2)
    bq = q_ref.shape[1]
    q = q_ref[0]
    steps = bq // bk

    def scores(kb):
        rows = pl.ds(pl.multiple_of(kb * bk, bk), bk)
        k = jnp.concatenate([kv_ref[0, rows, :MLA_NOPE],
                             kpe_ref[0, rows, :].astype(BF16)], axis=1)
        s = lax.dot_general(q, k, (((1,), (1,)), ((), ())), preferred_element_type=F32)
        return s, kv_ref[0, rows, MLA_NOPE:]

    def update(carry, s, v):
        m_i, l_i, acc = carry
        m_new = jnp.maximum(m_i, jnp.max(s, axis=-1, keepdims=True))
        alpha = jnp.exp(m_i - m_new)
        p = jnp.exp(s - m_new)
        l_new = alpha * l_i + jnp.sum(p, axis=-1, keepdims=True)
        acc = alpha * acc + jnp.dot(p.astype(BF16), v, preferred_element_type=F32)
        return m_new, l_new, acc

    def full_block(kb, carry):
        s, v = scores(kb)
        return update(carry, s, v)

    init = (jnp.full((bq, 1), NEG_INF, F32), jnp.zeros((bq, 1), F32),
            jnp.zeros((bq, MLA_V), F32))
    carry = lax.fori_loop(0, qi * steps, full_block, init)
    rq = lax.broadcasted_iota(jnp.int32, (bq, bk), 0)
    ck = lax.broadcasted_iota(jnp.int32, (bq, bk), 1)
    for t in range(steps):
        s, v = scores(qi * steps + t)
        s = jnp.where(ck + t * bk <= rq, s, NEG_INF)
        carry = update(carry, s, v)
    _, l_i, acc = carry
    o_ref[0] = (acc / l_i).astype(o_ref.dtype)


def mla_attention(q, kv, c, *, bsz, seq, kpe_block):
    hq = MLA_NOPE + V7X_LANES
    heads = q.shape[1] // hq
    bq = _tile(seq, MLA_BQ)
    bk = _tile(bq, MLA_BK)
    q3 = q.reshape(bsz, seq, heads * hq)
    kv3 = kv.reshape(bsz, seq, kv.shape[1])
    c3 = c.reshape(bsz, seq, c.shape[1])
    o = pl.pallas_call(
        functools.partial(_mla_attn_kernel, bk=bk),
        out_shape=jax.ShapeDtypeStruct((bsz, seq, heads * MLA_V), BF16),
        grid=(bsz, heads, seq // bq),
        in_specs=[pl.BlockSpec((1, bq, hq), lambda b, h, i: (b, i, h)),
                  pl.BlockSpec((1, seq, MLA_NOPE + MLA_V), lambda b, h, i: (b, 0, h)),
                  pl.BlockSpec((1, seq, V7X_LANES), lambda b, h, i: (b, 0, kpe_block))],
        out_specs=pl.BlockSpec((1, bq, MLA_V), lambda b, h, i: (b, i, h)),
        compiler_params=_params("parallel", "parallel", "parallel"),
        name="mla_attn",
    )(q3, kv3, c3)
    return o.reshape(bsz * seq, heads * MLA_V)


def _sgu_out_kernel(u_ref, v_ref, st_ref, lng_ref, lnb_ref, ws_ref, bs_ref, wo_ref, bo_ref,
                    h_ref, o_ref, acc_ref, gate_ref, *, width):
    g = pl.program_id(1)
    ng = pl.num_programs(1)
    bm = u_ref.shape[0]
    ch = ws_ref.shape[1]
    st = st_ref[...]
    mu = st[:, 0:1] * (1.0 / width)
    var = st[:, 1:2] * (1.0 / width) - mu * mu
    rstd = lax.rsqrt(var + NORM_EPS)
    ti = lax.broadcasted_iota(jnp.int32, (ch, ch), 0)
    si = lax.broadcasted_iota(jnp.int32, (ch, ch), 1)
    ws = jnp.where(si <= ti, ws_ref[0], 0.0).astype(BF16)
    bs = bs_ref[0]
    lng = lng_ref[...]
    lnb = lnb_ref[...]
    for c in range(bm // ch):
        rows = slice(c * ch, (c + 1) * ch)
        vn = (v_ref[rows, :].astype(F32) - mu[rows]) * rstd[rows] * lng + lnb
        mixed = jnp.dot(ws, vn.astype(BF16), preferred_element_type=F32) + bs
        gate_ref[rows, :] = (u_ref[rows, :].astype(F32) * mixed).astype(BF16)
    part = jnp.dot(gate_ref[...], wo_ref[...], preferred_element_type=F32)

    @pl.when(g == 0)
    def _():
        acc_ref[...] = part

    @pl.when(g > 0)
    def _():
        acc_ref[...] += part

    @pl.when(g == ng - 1)
    def _():
        o_ref[...] = acc_ref[...] + bo_ref[...] + h_ref[...]


def sgu_out(u, v, st, ln_g, ln_b, w_s, b_s, w_out, b_out, h):
    m, e = u.shape
    d = w_out.shape[1]
    groups, ch, _ = w_s.shape
    gw = e // groups
    bm = _tile(m, SGU_BM)
    assert bm % ch == 0
    return pl.pallas_call(
        functools.partial(_sgu_out_kernel, width=e),
        out_shape=jax.ShapeDtypeStruct((m, d), F32),
        grid=(m // bm, groups),
        in_specs=[
            pl.BlockSpec((bm, gw), lambda i, g: (i, g)),
            pl.BlockSpec((bm, gw), lambda i, g: (i, g)),
            pl.BlockSpec((bm, V7X_LANES), lambda i, g: (i, 0)),
            pl.BlockSpec((1, gw), lambda i, g: (0, g)),
            pl.BlockSpec((1, gw), lambda i, g: (0, g)),
            pl.BlockSpec((1, ch, ch), lambda i, g: (g, 0, 0)),
            pl.BlockSpec((1, ch, 1), lambda i, g: (g, 0, 0)),
            pl.BlockSpec((gw, d), lambda i, g: (g, 0)),
            pl.BlockSpec((1, d), lambda i, g: (0, 0)),
            pl.BlockSpec((bm, d), lambda i, g: (i, 0)),
        ],
        out_specs=pl.BlockSpec((bm, d), lambda i, g: (i, 0)),
        scratch_shapes=[pltpu.VMEM((bm, d), F32), pltpu.VMEM((bm, gw), BF16)],
        compiler_params=_params("parallel", "arbitrary"),
        name="sgu_out",
    )(u, v, st, ln_g.reshape(1, e).astype(F32), ln_b.reshape(1, e).astype(F32),
      w_s.astype(F32), b_s.reshape(groups, ch, 1).astype(F32), w_out.astype(BF16),
      b_out.reshape(1, d).astype(F32), h)


def _rope_tables(seq, dim):
    half = dim // 2
    pos = jnp.arange(seq, dtype=F32)
    inv = ROPE_THETA ** (-(jnp.arange(0, dim, 2, dtype=F32) / dim))
    ang = pos[:, None] * inv[None, :]
    zeros = jnp.zeros((seq, V7X_LANES // 2 - half), F32)
    cos = jnp.concatenate([jnp.cos(ang), zeros, jnp.cos(ang), zeros], axis=1)
    sin = jnp.concatenate([-jnp.sin(ang), zeros, jnp.sin(ang), zeros], axis=1)
    return cos, sin


def _spread_rope_cols(w):
    k, dim = w.shape
    half = dim // 2
    zeros = jnp.zeros((k, V7X_LANES // 2 - half), w.dtype)
    return jnp.concatenate([w[:, :half], zeros, w[:, half:], zeros], axis=1)


def conformer_mixer(xn, h, w_in, b_in, dw_w, dw_b, ln_g, ln_b, w_out, b_out, *, bsz, seq):
    c = w_out.shape[0]
    wi = w_in.astype(BF16)
    z = matmul(xn, wi[:, :c], w2=wi[:, c:], bias=b_in[:c], bias2=b_in[c:])
    z = causal_dwconv(z.reshape(bsz, seq, c), dw_w, dw_b).reshape(bsz * seq, c)
    return matmul(z, w_out.astype(BF16), prologue="ln_silu", pg=ln_g, pb=ln_b, bias=b_out,
                  residual=h, out_dtype=F32)


def dilated_attention_mixer(xn, h, w_qkv, w_o, *, bsz, seq):
    d = xn.shape[1]
    dh = DSA_HEAD_DIM
    heads = w_o.shape[0] // dh
    groups = len(DSA_CONFIGS)
    hw = heads * dh
    w5 = w_qkv.reshape(d, groups, 3, hw)
    wq = w5[:, :, 0] * (dh ** -0.5)
    w_qk = jnp.stack([wq, w5[:, :, 1]], axis=2).reshape(d, groups * 2 * hw).astype(BF16)
    w_v = w5[:, :, 2].reshape(d, groups * hw).astype(BF16)
    bn = _tile(w_qk.shape[1], 1024)
    qk = matmul(xn, w_qk, rope=_rope_tables(seq, dh), rope_mask=(True,) * (bn // V7X_LANES),
                seq=seq, bn=bn)
    v = matmul(xn, w_v)
    outs, lses = [], []
    for g, (window, dil) in enumerate(DSA_CONFIGS):
        o, lse = band_attention(qk, v, g, window, dil, bsz=bsz, seq=seq, heads=heads)
        outs.append(o)
        lses.append(lse)
    o = merge_groups(outs, lses, heads=heads)
    return matmul(o, w_o.astype(BF16), residual=h, out_dtype=F32)


def mla_mixer(xn, h, w_in, q_norm, w_qb, kv_norm, w_kvb, w_o, *, bsz, seq):
    q_rank = q_norm.shape[0]
    kv_rank = kv_norm.shape[0]
    assert q_rank == kv_rank
    heads = MLA_HEADS
    w_in_p = jnp.concatenate(
        [w_in[:, :q_rank + kv_rank], _spread_rope_cols(w_in[:, q_rank + kv_rank:])],
        axis=1).astype(BF16)
    n_c = w_in_p.shape[1]
    kpe_group = (q_rank + kv_rank) // V7X_LANES
    cmask = tuple(k == kpe_group for k in range(n_c // V7X_LANES))
    tables = _rope_tables(seq, MLA_ROPE)
    c = matmul(xn, w_in_p, rope=tables, rope_mask=cmask, seq=seq, out_dtype=F32, bn=n_c)
    wq3 = w_qb.reshape(q_rank, heads, MLA_NOPE + MLA_ROPE) * ((MLA_NOPE + MLA_ROPE) ** -0.5)
    wq_pe = jax.vmap(_spread_rope_cols, in_axes=1, out_axes=1)(wq3[:, :, MLA_NOPE:])
    wq_p = jnp.concatenate([wq3[:, :, :MLA_NOPE], wq_pe], axis=2)
    wq_p = wq_p.reshape(q_rank, heads * (MLA_NOPE + V7X_LANES)).astype(BF16)
    bn = _tile(wq_p.shape[1], 1024)
    qmask = tuple(k % 2 == 1 for k in range(bn // V7X_LANES))
    q = matmul(c, wq_p, x_cols=(0, q_rank), prologue="rms", pg=q_norm, rope=tables,
               rope_mask=qmask, seq=seq, bn=bn)
    kv = matmul(c, w_kvb.astype(BF16), x_cols=(1, kv_rank), prologue="rms", pg=kv_norm)
    o = mla_attention(q, kv, c, bsz=bsz, seq=seq, kpe_block=kpe_group)
    return matmul(o, w_o.astype(BF16), residual=h, out_dtype=F32)


def sgu_mixer(xn, h, w_in, b_in, ln_g, ln_b, w_s, b_s, w_out, b_out):
    e = w_out.shape[0]
    wi = w_in.astype(BF16)
    u = matmul(xn, wi[:, :e], bias=b_in[:e], gelu=True)
    v, st = matmul(xn, wi[:, e:], bias=b_in[e:], gelu=True, stats=True)
    return sgu_out(u, v, st, ln_g, ln_b, w_s, b_s, w_out, b_out, h)


def kernel(x, l0_norm_mix, l0_cc_w_in, l0_cc_b_in, l0_cc_dw_w, l0_cc_dw_b, l0_cc_ln_g, l0_cc_ln_b, l0_cc_w_out, l0_cc_b_out, l0_norm_ffn, l0_ffn_w_up, l0_ffn_dw_w, l0_ffn_dw_b, l0_ffn_w_down, l1_norm_mix, l1_dsa_w_qkv, l1_dsa_w_o, l1_norm_ffn, l1_ffn_w_up, l1_ffn_dw_w, l1_ffn_dw_b, l1_ffn_w_down, l2_norm_mix, l2_mla_w_in, l2_mla_q_norm, l2_mla_w_qb, l2_mla_kv_norm, l2_mla_w_kvb, l2_mla_w_o, l2_norm_ffn, l2_ffn_w_up, l2_ffn_dw_w, l2_ffn_dw_b, l2_ffn_w_down, l3_norm_mix, l3_sg_w_in, l3_sg_b_in, l3_sg_ln_g, l3_sg_ln_b, l3_sg_w_s, l3_sg_b_s, l3_sg_w_out, l3_sg_b_out, l3_norm_ffn, l3_ffn_w_up, l3_ffn_dw_w, l3_ffn_dw_b, l3_ffn_w_down, final_norm):
    bsz, seq, d = x.shape
    dims = dict(bsz=bsz, seq=seq)
    h = x.reshape(bsz * seq, d)

    def ffn(h, norm, w_up, dw_w, dw_b, w_down):
        return conv_ffn(rmsnorm(h, norm, BF16), h, w_up, dw_w, dw_b, w_down, seq=seq)

    h = conformer_mixer(rmsnorm(h, l0_norm_mix, BF16), h, l0_cc_w_in, l0_cc_b_in, l0_cc_dw_w,
                        l0_cc_dw_b, l0_cc_ln_g, l0_cc_ln_b, l0_cc_w_out, l0_cc_b_out, **dims)
    h = ffn(h, l0_norm_ffn, l0_ffn_w_up, l0_ffn_dw_w, l0_ffn_dw_b, l0_ffn_w_down)
    h = dilated_attention_mixer(rmsnorm(h, l1_norm_mix, BF16), h, l1_dsa_w_qkv, l1_dsa_w_o,
                                **dims)
    h = ffn(h, l1_norm_ffn, l1_ffn_w_up, l1_ffn_dw_w, l1_ffn_dw_b, l1_ffn_w_down)
    h = mla_mixer(rmsnorm(h, l2_norm_mix, BF16), h, l2_mla_w_in, l2_mla_q_norm, l2_mla_w_qb,
                  l2_mla_kv_norm, l2_mla_w_kvb, l2_mla_w_o, **dims)
    h = ffn(h, l2_norm_ffn, l2_ffn_w_up, l2_ffn_dw_w, l2_ffn_dw_b, l2_ffn_w_down)
    h = sgu_mixer(rmsnorm(h, l3_norm_mix, BF16), h, l3_sg_w_in, l3_sg_b_in, l3_sg_ln_g,
                  l3_sg_ln_b, l3_sg_w_s, l3_sg_b_s, l3_sg_w_out, l3_sg_b_out)
    h = ffn(h, l3_norm_ffn, l3_ffn_w_up, l3_ffn_dw_w, l3_ffn_dw_b, l3_ffn_w_down)
    return rmsnorm(h, final_norm, F32).reshape(bsz, seq, d)
```

```python
import functools
import math

import jax
import jax.numpy as jnp
from jax import lax
from jax.experimental import pallas as pl
from jax.experimental.pallas import tpu as pltpu

F32 = jnp.float32
BF16 = jnp.bfloat16

NORM_EPS = 1e-6
ROPE_THETA = 10000.0
NEG_INF = -1e30

DSA_HEAD_DIM = 128
DSA_CONFIGS = ((128, 1), (512, 4), (2048, 16))
DSA_BLOCK = 128
MLA_HEADS = 16
MLA_NOPE = 128
MLA_ROPE = 64
MLA_V = 128
SG_CHUNK = 128
SG_GROUPS = 8

V7X_LANES = 128
V7X_SUBLANES = 8
V7X_VMEM_LIMIT_BYTES = 56 * 1024 * 1024

MM_BM, MM_BN = 1024, 1024
MM_PROLOGUE_BM = 512
FFN_BM, FFN_FC = 512, 512
FFN_SUB, FFN_RSUB = 256, 256
MLA_BQ, MLA_BK = 512, 512
SGU_BM = 512
CONV_CB, CONV_ROWS = 256, 64
DSA_PROJ_BN = 512


def _params(*semantics):
    return pltpu.CompilerParams(dimension_semantics=semantics,
                                vmem_limit_bytes=V7X_VMEM_LIMIT_BYTES)


def _sigmoid(x):
    return 1.0 / (1.0 + jnp.exp(-x))


def _tile(n, want):
    t = min(n, want)
    while n % t:
        t -= 1
    return t


def _rmsnorm_kernel(x_ref, g_ref, o_ref):
    x = x_ref[...].astype(F32)
    y = x * lax.rsqrt(jnp.mean(x * x, axis=-1, keepdims=True) + NORM_EPS)
    o_ref[...] = (y * g_ref[...]).astype(o_ref.dtype)


def rmsnorm(x, g, out_dtype):
    m, d = x.shape
    bm = _tile(m, 512)
    return pl.pallas_call(
        _rmsnorm_kernel,
        out_shape=jax.ShapeDtypeStruct((m, d), out_dtype),
        grid=(m // bm,),
        in_specs=[pl.BlockSpec((bm, d), lambda i: (i, 0)),
                  pl.BlockSpec((1, d), lambda i: (0, 0))],
        out_specs=pl.BlockSpec((bm, d), lambda i: (i, 0)),
        compiler_params=_params("parallel"),
        name="rmsnorm",
    )(x, g.reshape(1, d).astype(F32))


def _mm_kernel(*refs, prologue, has_bias, glu, gelu, rope_mask, residual, stats):
    it = iter(refs)
    x_ref = next(it)
    pg_ref = next(it) if prologue else None
    pb_ref = next(it) if prologue == "ln_silu" else None
    w_ref = next(it)
    w2_ref = next(it) if glu else None
    b_ref = next(it) if has_bias else None
    b2_ref = next(it) if (glu and has_bias) else None
    cos_ref = next(it) if rope_mask else None
    sin_ref = next(it) if rope_mask else None
    res_ref = next(it) if residual else None
    o_ref = next(it)
    st_ref = next(it) if stats else None
    xs_ref = next(it) if prologue else None

    j = pl.program_id(1)

    if prologue:
        @pl.when(j == 0)
        def _():
            xf = x_ref[...].astype(F32)
            if prologue == "rms":
                y = xf * lax.rsqrt(jnp.mean(xf * xf, axis=-1, keepdims=True) + NORM_EPS)
                y = y * pg_ref[...]
            else:
                mu = jnp.mean(xf, axis=-1, keepdims=True)
                xc = xf - mu
                y = xc * lax.rsqrt(jnp.mean(xc * xc, axis=-1, keepdims=True) + NORM_EPS)
                y = y * pg_ref[...] + pb_ref[...]
                y = y * _sigmoid(y)
            xs_ref[...] = y.astype(BF16)
        xv = xs_ref[...]
    else:
        xv = x_ref[...]

    acc = jnp.dot(xv, w_ref[...], preferred_element_type=F32)
    if has_bias:
        acc = acc + b_ref[...]
    if glu:
        gate = jnp.dot(xv, w2_ref[...], preferred_element_type=F32)
        if has_bias:
            gate = gate + b2_ref[...]
        acc = acc * _sigmoid(gate)
    if gelu:
        acc = 0.5 * acc * (1.0 + lax.erf(acc * (2.0 ** -0.5)))
    if stats:
        s1 = jnp.sum(acc, axis=-1, keepdims=True)
        s2 = jnp.sum(acc * acc, axis=-1, keepdims=True)
        lane = lax.broadcasted_iota(jnp.int32, st_ref.shape, 1)
        upd = jnp.where(lane == 0, s1, jnp.where(lane == 1, s2, 0.0))

        @pl.when(j == 0)
        def _():
            st_ref[...] = upd

        @pl.when(j > 0)
        def _():
            st_ref[...] += upd
    if residual:
        acc = acc + res_ref[...]
    if rope_mask:
        cos = cos_ref[...]
        sin = sin_ref[...]
        for k, on in enumerate(rope_mask):
            sl = acc[:, k * V7X_LANES:(k + 1) * V7X_LANES]
            if on:
                sl = sl * cos + pltpu.roll(sl, V7X_LANES // 2, 1) * sin
            o_ref[:, k * V7X_LANES:(k + 1) * V7X_LANES] = sl.astype(o_ref.dtype)
    else:
        o_ref[...] = acc.astype(o_ref.dtype)


def matmul(x, w, *, x_cols=None, prologue=None, pg=None, pb=None, w2=None, bias=None,
           bias2=None, gelu=False, rope=None, rope_mask=None, seq=None, residual=None,
           stats=False, out_dtype=BF16, bn=None):
    m = x.shape[0]
    k, n = w.shape
    kblk = 0 if x_cols is None else x_cols[0]
    bm = _tile(m, MM_PROLOGUE_BM if prologue else MM_BM)
    bn = _tile(n, bn or MM_BN)
    glu = w2 is not None
    has_bias = bias is not None
    if rope_mask:
        assert bn % V7X_LANES == 0 and len(rope_mask) == bn // V7X_LANES
        assert seq % bm == 0
    args, specs = [x], [pl.BlockSpec((bm, k), lambda i, j: (i, kblk))]
    if prologue:
        args.append(pg.reshape(1, k).astype(F32))
        specs.append(pl.BlockSpec((1, k), lambda i, j: (0, 0)))
        if prologue == "ln_silu":
            args.append(pb.reshape(1, k).astype(F32))
            specs.append(pl.BlockSpec((1, k), lambda i, j: (0, 0)))
    args.append(w)
    specs.append(pl.BlockSpec((k, bn), lambda i, j: (0, j)))
    if glu:
        args.append(w2)
        specs.append(pl.BlockSpec((k, bn), lambda i, j: (0, j)))
    if has_bias:
        args.append(bias.reshape(1, n).astype(F32))
        specs.append(pl.BlockSpec((1, bn), lambda i, j: (0, j)))
        if glu:
            args.append(bias2.reshape(1, n).astype(F32))
            specs.append(pl.BlockSpec((1, bn), lambda i, j: (0, j)))
    if rope_mask:
        tiles_per_seq = seq // bm
        for t in rope:
            args.append(t)
            specs.append(pl.BlockSpec((bm, V7X_LANES), lambda i, j: (i % tiles_per_seq, 0)))
    if residual is not None:
        args.append(residual)
        specs.append(pl.BlockSpec((bm, bn), lambda i, j: (i, j)))
    out_shape = [jax.ShapeDtypeStruct((m, n), out_dtype)]
    out_specs = [pl.BlockSpec((bm, bn), lambda i, j: (i, j))]
    if stats:
        out_shape.append(jax.ShapeDtypeStruct((m, V7X_LANES), F32))
        out_specs.append(pl.BlockSpec((bm, V7X_LANES), lambda i, j: (i, 0)))
    scratch = [pltpu.VMEM((bm, k), BF16)] if prologue else []
    kern = functools.partial(
        _mm_kernel, prologue=prologue, has_bias=has_bias, glu=glu, gelu=gelu,
        rope_mask=tuple(rope_mask) if rope_mask else None,
        residual=residual is not None, stats=stats)
    out = pl.pallas_call(
        kern,
        out_shape=out_shape,
        grid=(m // bm, n // bn),
        in_specs=specs,
        out_specs=out_specs,
        scratch_shapes=scratch,
        compiler_params=_params("parallel", "arbitrary"),
        name="matmul",
    )(*args)
    return out if stats else out[0]


def _dwconv_kernel(x_ref, w_ref, b_ref, o_ref, sh_ref, *, width, row_chunk):
    s, cb = x_ref.shape[1], x_ref.shape[2]
    pad = sh_ref.shape[1] - s
    x = x_ref[0].astype(F32)
    xz = jnp.concatenate([x, jnp.zeros((V7X_SUBLANES, cb), F32)], axis=0)
    zero_top = jnp.zeros((pad, cb), F32)
    for r in range(V7X_SUBLANES):
        sh_ref[r, :pad, :] = zero_top
        shifted = xz if r == 0 else pltpu.roll(xz, r, 0)
        sh_ref[r, pad:, :] = shifted[:s]

    w = w_ref[...]
    bias = b_ref[...]

    def chunk(c, carry):
        base = pl.multiple_of(c * row_chunk, row_chunk)
        acc = jnp.zeros((row_chunk, cb), F32) + bias
        for kk in range(width):
            shift = width - 1 - kk
            q, r = divmod(shift, V7X_SUBLANES)
            start = pl.multiple_of(base + (pad - q * V7X_SUBLANES), V7X_SUBLANES)
            tap = sh_ref[r, pl.ds(start, row_chunk), :]
            acc = acc + tap * w[kk:kk + 1, :]
        o_ref[0, pl.ds(base, row_chunk), :] = acc.astype(o_ref.dtype)
        return carry

    lax.fori_loop(0, s // row_chunk, chunk, 0)


def causal_dwconv(x, w, b):
    bsz, s, c = x.shape
    width = w.shape[0]
    cb = _tile(c, CONV_CB)
    row_chunk = _tile(s, CONV_ROWS)
    pad = -(-(width - 1) // V7X_SUBLANES) * V7X_SUBLANES
    return pl.pallas_call(
        functools.partial(_dwconv_kernel, width=width, row_chunk=row_chunk),
        out_shape=jax.ShapeDtypeStruct((bsz, s, c), BF16),
        grid=(bsz, c // cb),
        in_specs=[pl.BlockSpec((1, s, cb), lambda i, j: (i, 0, j)),
                  pl.BlockSpec((width, cb), lambda i, j: (0, j)),
                  pl.BlockSpec((1, cb), lambda i, j: (0, j))],
        out_specs=pl.BlockSpec((1, s, cb), lambda i, j: (i, 0, j)),
        scratch_shapes=[pltpu.VMEM((V7X_SUBLANES, pad + s, cb), F32)],
        compiler_params=_params("parallel", "parallel"),
        name="dwconv",
    )(x, w.astype(F32), b.reshape(1, c).astype(F32))


def _ffn_kernel(h_ref, ng_ref, wg_ref, wa_ref, cwg_ref, cwa_ref, cbg_ref, cba_ref, wd_ref,
                o_ref, xs_ref, acc_ref, perm_ref, carry_g_ref, carry_a_ref, *, tiles_per_seq,
                sub, rsub):
    i = pl.program_id(0)
    j = pl.program_id(1)
    nj = pl.num_programs(1)
    bm, d = h_ref.shape
    fc = wg_ref.shape[1]
    seq_start = (i % tiles_per_seq) == 0
    sl = V7X_SUBLANES
    nv = rsub // sl
    slabs = [slice(c * V7X_LANES, (c + 1) * V7X_LANES) for c in range(d // V7X_LANES)]

    na = nv // sl

    def staged(r, a, k):
        return pl.ds(r * rsub + sl * sl * a + k, sl, stride=sl)

    @pl.when(seq_start)
    def _():
        carry_g_ref[j] = jnp.zeros((2 * sl, fc), F32)
        carry_a_ref[j] = jnp.zeros((2 * sl, fc), F32)

    @pl.when(j == 0)
    def _():
        hv = h_ref[...]
        y = hv * lax.rsqrt(jnp.mean(hv * hv, axis=-1, keepdims=True) + NORM_EPS) * ng_ref[...]
        for r in range(bm // rsub):
            for s in range(sl):
                for a in range(na):
                    src = r * rsub + nv * s + sl * a
                    dst = r * rsub + sl * sl * a + sl * s
                    for c, cols in enumerate(slabs):
                        perm_ref[c, dst:dst + sl, :] = y[src:src + sl, cols]
        for r in range(bm // rsub):
            for a in range(na):
                for b in range(0, sl, 2):
                    dst = r * rsub + sl * (sl * a + b)
                    for c, cols in enumerate(slabs):
                        pair = jnp.concatenate([perm_ref[c, staged(r, a, b), :],
                                                perm_ref[c, staged(r, a, b + 1), :]], axis=0)
                        xs_ref[dst:dst + 2 * sl, cols] = pair.astype(BF16)
        acc_ref[...] = jnp.zeros(acc_ref.shape, F32)

    first_row = lax.broadcasted_iota(jnp.int32, (sl, sub), 0) == 0

    def shift1(z, prev_row):
        top = jnp.where(first_row, prev_row, pltpu.roll(z[rsub - sl:, :], 1, 0))
        return jnp.concatenate([top, z[:rsub - sl, :]], axis=0)

    def conv(z, prev, cw, cb):
        z1 = shift1(z, prev[2 * sl - 1:2 * sl, :])
        z2 = shift1(z1, prev[sl - 1:sl, :])
        return z2 * cw[0:1, :] + z1 * cw[1:2, :] + z * cw[2:3, :] + cb

    tiles = [(slice(r * rsub, (r + 1) * rsub), slice(c * sub, (c + 1) * sub))
             for c in range(fc // sub) for r in range(bm // rsub)]

    def up(rows, cols):
        xr = xs_ref[rows, :]
        return (jnp.dot(xr, wg_ref[:, cols], preferred_element_type=F32),
                jnp.dot(xr, wa_ref[:, cols], preferred_element_type=F32))

    z_next = up(*tiles[0])
    for k, (rows, cols) in enumerate(tiles):
        zg, za = z_next
        if k + 1 < len(tiles):
            z_next = up(*tiles[k + 1])
        if rows.start == 0:
            prev_g = carry_g_ref[j, :, cols]
            prev_a = carry_a_ref[j, :, cols]
        g = conv(zg, prev_g, cwg_ref[:, cols], cbg_ref[:, cols])
        a = conv(za, prev_a, cwa_ref[:, cols], cba_ref[:, cols])
        prev_g = zg[rsub - 2 * sl:, :]
        prev_a = za[rsub - 2 * sl:, :]
        if rows.stop == bm:
            carry_g_ref[j, :, cols] = prev_g
            carry_a_ref[j, :, cols] = prev_a
        hg = 0.5 * g
        act = ((hg + hg * jnp.tanh(hg)) * a).astype(BF16)
        acc_ref[rows, :] += jnp.dot(act, wd_ref[cols, :], preferred_element_type=F32)

    @pl.when(j == nj - 1)
    def _():
        for c, cols in enumerate(slabs):
            perm_ref[c] = acc_ref[:, cols]
        for r in range(bm // rsub):
            for s in range(sl):
                for a in range(na):
                    dst = r * rsub + nv * s + sl * a
                    for c, cols in enumerate(slabs):
                        o_ref[dst:dst + sl, cols] = (perm_ref[c, staged(r, a, s), :]
                                                     + h_ref[dst:dst + sl, cols])


def conv_ffn(h, norm_g, w_up, dw_w, dw_b, w_down, *, seq):
    m, d = h.shape
    f = w_down.shape[0]
    bm = _tile(seq, FFN_BM)
    fc = _tile(f, FFN_FC)
    nf = f // fc
    wu = w_up.astype(BF16)
    wd = w_down.astype(BF16)
    cw = dw_w.astype(F32)
    cb = dw_b.reshape(1, 2 * f).astype(F32)
    rsub = _tile(bm, FFN_RSUB)
    assert rsub % (V7X_SUBLANES * V7X_SUBLANES) == 0 and d % V7X_LANES == 0
    kern = functools.partial(_ffn_kernel, tiles_per_seq=seq // bm, sub=_tile(fc, FFN_SUB),
                             rsub=rsub)
    return pl.pallas_call(
        kern,
        out_shape=jax.ShapeDtypeStruct((m, d), F32),
        grid=(m // bm, nf),
        in_specs=[
            pl.BlockSpec((bm, d), lambda i, j: (i, 0)),
            pl.BlockSpec((1, d), lambda i, j: (0, 0)),
            pl.BlockSpec((d, fc), lambda i, j: (0, j)),
            pl.BlockSpec((d, fc), lambda i, j: (0, j + nf)),
            pl.BlockSpec((3, fc), lambda i, j: (0, j)),
            pl.BlockSpec((3, fc), lambda i, j: (0, j + nf)),
            pl.BlockSpec((1, fc), lambda i, j: (0, j)),
            pl.BlockSpec((1, fc), lambda i, j: (0, j + nf)),
            pl.BlockSpec((fc, d), lambda i, j: (j, 0)),
        ],
        out_specs=pl.BlockSpec((bm, d), lambda i, j: (i, 0)),
        scratch_shapes=[
            pltpu.VMEM((bm, d), BF16),
            pltpu.VMEM((bm, d), F32),
            pltpu.VMEM((d // V7X_LANES, bm, V7X_LANES), F32),
            pltpu.VMEM((nf, 2 * V7X_SUBLANES, fc), F32),
            pltpu.VMEM((nf, 2 * V7X_SUBLANES, fc), F32),
        ],
        compiler_params=_params("arbitrary", "arbitrary"),
        name="conv_ffn",
    )(h, norm_g.reshape(1, d).astype(F32), wu, wu, cw, cw, cb, cb, wd)


def _band_attn_kernel(q_ref, kp_ref, kc_ref, vp_ref, vc_ref, o_ref, lse_ref, *, heads, span):
    n = pl.program_id(2)
    blk = q_ref.shape[1]
    qi = lax.broadcasted_iota(jnp.int32, (blk, 2 * blk), 0)
    kj = lax.broadcasted_iota(jnp.int32, (blk, 2 * blk), 1)
    dist = blk + qi - kj
    first_key = jnp.where(n > 0, 0, blk)
    mask = (dist >= 0) & (dist <= span) & (kj >= first_key)
    lane = lax.broadcasted_iota(jnp.int32, (blk, V7X_LANES), 1)
    lse_tile = jnp.zeros((blk, V7X_LANES), F32)
    dh = DSA_HEAD_DIM
    for hh in range(heads):
        cols = slice(hh * dh, (hh + 1) * dh)
        q = q_ref[0, :, cols]
        k = jnp.concatenate([kp_ref[0, :, cols], kc_ref[0, :, cols]], axis=0)
        v = jnp.concatenate([vp_ref[0, :, cols], vc_ref[0, :, cols]], axis=0)
        s = lax.dot_general(q, k, (((1,), (1,)), ((), ())), preferred_element_type=F32)
        s = jnp.where(mask, s, NEG_INF)
        mx = jnp.max(s, axis=-1, keepdims=True)
        p = jnp.exp(s - mx)
        l = jnp.sum(p, axis=-1, keepdims=True)
        o = jnp.dot(p.astype(BF16), v, preferred_element_type=F32) / l
        o_ref[0, :, cols] = o.astype(o_ref.dtype)
        lse_tile = jnp.where(lane == hh, mx + jnp.log(l), lse_tile)
    lse_ref[0] = lse_tile


def band_attention(qkv, window, dil, *, bsz, seq, heads):
    hw = heads * DSA_HEAD_DIM
    blk = DSA_BLOCK
    nb = seq // dil // blk
    span = window // dil
    qkv3 = qkv.reshape(bsz, seq, 3 * hw)
    cur = lambda r, n: r * nb + n
    prev = lambda r, n: r * nb + jnp.maximum(n - 1, 0)
    return pl.pallas_call(
        functools.partial(_band_attn_kernel, heads=heads, span=span),
        out_shape=[jax.ShapeDtypeStruct((bsz, seq, hw), BF16),
                   jax.ShapeDtypeStruct((bsz, seq, V7X_LANES), F32)],
        grid=(bsz, dil, nb),
        in_specs=[
            pl.BlockSpec((1, blk, hw), lambda b, r, n: (b, cur(r, n), 0)),
            pl.BlockSpec((1, blk, hw), lambda b, r, n: (b, prev(r, n), 1)),
            pl.BlockSpec((1, blk, hw), lambda b, r, n: (b, cur(r, n), 1)),
            pl.BlockSpec((1, blk, hw), lambda b, r, n: (b, prev(r, n), 2)),
            pl.BlockSpec((1, blk, hw), lambda b, r, n: (b, cur(r, n), 2)),
        ],
        out_specs=[pl.BlockSpec((1, blk, hw), lambda b, r, n: (b, cur(r, n), 0)),
                   pl.BlockSpec((1, blk, V7X_LANES), lambda b, r, n: (b, cur(r, n), 0))],
        compiler_params=_params("parallel", "parallel", "parallel"),
        name="band_attn",
    )(qkv3, qkv3, qkv3, qkv3, qkv3)


def _dsa_proj_kernel(x_ref, w_ref, cos_ref, sin_ref, o_ref, *scratch, dil, rope_tiles):
    j = pl.program_id(1)
    s = x_ref.shape[0]
    ln = s // dil
    acc = jnp.dot(x_ref[...], w_ref[...], preferred_element_type=F32)
    slabs = [slice(c * V7X_LANES, (c + 1) * V7X_LANES) for c in range(acc.shape[1] // V7X_LANES)]
    if dil > 1:
        stage_ref, = scratch
        for c, cols in enumerate(slabs):
            stage_ref[c] = acc[:, cols]

    def emit(rope):
        for r in range(dil):
            rows = slice(r * ln, (r + 1) * ln)
            for c, cols in enumerate(slabs):
                blk = stage_ref[c, pl.ds(r, ln, stride=dil), :] if dil > 1 else acc[:, cols]
                if rope:
                    blk = (blk * cos_ref[rows, :]
                           + pltpu.roll(blk, V7X_LANES // 2, 1) * sin_ref[rows, :])
                o_ref[rows, cols] = blk.astype(o_ref.dtype)

    @pl.when(j < rope_tiles)
    def _():
        emit(True)

    @pl.when(j >= rope_tiles)
    def _():
        emit(False)


def dsa_project(xn, w_g, tables, dil, *, bsz, seq, rope_cols):
    m, d = xn.shape
    n = w_g.shape[1]
    bn = _tile(n, DSA_PROJ_BN)
    assert rope_cols % bn == 0
    residue_major = lambda t: t.reshape(seq // dil, dil, V7X_LANES).transpose(1, 0, 2).reshape(
        seq, V7X_LANES)
    cos, sin = (residue_major(t) for t in tables)
    scratch = [pltpu.VMEM((bn // V7X_LANES, seq, V7X_LANES), F32)] if dil > 1 else []
    return pl.pallas_call(
        functools.partial(_dsa_proj_kernel, dil=dil, rope_tiles=rope_cols // bn),
        out_shape=jax.ShapeDtypeStruct((m, n), BF16),
        grid=(bsz, n // bn),
        in_specs=[pl.BlockSpec((seq, d), lambda b, j: (b, 0)),
                  pl.BlockSpec((d, bn), lambda b, j: (0, j)),
                  pl.BlockSpec((seq, V7X_LANES), lambda b, j: (0, 0)),
                  pl.BlockSpec((seq, V7X_LANES), lambda b, j: (0, 0))],
        out_specs=pl.BlockSpec((seq, bn), lambda b, j: (b, j)),
        scratch_shapes=scratch,
        compiler_params=_params("parallel", "arbitrary"),
        name="dsa_proj",
    )(xn, w_g, cos, sin)


def _merge_kernel(*refs, dils, heads):
    groups = len(dils)
    o_refs = refs[:groups]
    lse_refs = refs[groups:2 * groups]
    out_ref, so_ref, sl_ref = refs[2 * groups:]
    s = out_ref.shape[1]

    def to_sequence_order(dst_ref, g, rows_of):
        ln = s // dils[g]
        for r in range(dils[g]):
            dst_ref[g, pl.ds(r, ln, stride=dils[g]), :] = rows_of(slice(r * ln, (r + 1) * ln))

    for g in range(groups):
        to_sequence_order(sl_ref, g, lambda rows, g=g: lse_refs[g][0, rows, :])
    lses = [sl_ref[g] for g in range(groups)]
    mx = functools.reduce(jnp.maximum, lses)
    es = [jnp.exp(x - mx) for x in lses]
    inv = 1.0 / functools.reduce(lambda a, b: a + b, es)
    alphas = [e * inv for e in es]
    dh = DSA_HEAD_DIM
    for hh in range(heads):
        cols = slice(hh * dh, (hh + 1) * dh)
        acc = None
        for g in range(groups):
            to_sequence_order(so_ref, g,
                              lambda rows, g=g: o_refs[g][0, rows, cols].astype(F32))
            t = alphas[g][:, hh:hh + 1] * so_ref[g]
            acc = t if acc is None else acc + t
        out_ref[0, :, cols] = acc.astype(out_ref.dtype)


def merge_groups(outs, lses, dils, *, heads):
    bsz, seq, hw = outs[0].shape
    groups = len(outs)
    assert DSA_HEAD_DIM == V7X_LANES
    out = pl.pallas_call(
        functools.partial(_merge_kernel, dils=tuple(dils), heads=heads),
        out_shape=jax.ShapeDtypeStruct((bsz, seq, hw), BF16),
        grid=(bsz,),
        in_specs=[pl.BlockSpec((1, seq, hw), lambda b: (b, 0, 0))] * groups
        + [pl.BlockSpec((1, seq, V7X_LANES), lambda b: (b, 0, 0))] * groups,
        out_specs=pl.BlockSpec((1, seq, hw), lambda b: (b, 0, 0)),
        scratch_shapes=[pltpu.VMEM((groups, seq, V7X_LANES), F32),
                        pltpu.VMEM((groups, seq, V7X_LANES), F32)],
        compiler_params=_params("parallel"),
        name="merge_groups",
    )(*outs, *lses)
    return out.reshape(bsz * seq, hw)


def _mla_attn_kernel(q_ref, kv_ref, kpe_ref, o_ref, *, bk):
    qi = pl.program_id(2)
    bq = q_ref.shape[1]
    q = q_ref[0]
    steps = bq // bk

    def scores(kb):
        rows = pl.ds(pl.multiple_of(kb * bk, bk), bk)
        k = jnp.concatenate([kv_ref[0, rows, :MLA_NOPE],
                             kpe_ref[0, rows, :].astype(BF16)], axis=1)
        s = lax.dot_general(q, k, (((1,), (1,)), ((), ())), preferred_element_type=F32)
        return s, kv_ref[0, rows, MLA_NOPE:]

    def update(carry, s, v):
        m_i, l_i, acc = carry
        m_new = jnp.maximum(m_i, jnp.max(s, axis=-1, keepdims=True))
        alpha = jnp.exp(m_i - m_new)
        p = jnp.exp(s - m_new)
        l_new = alpha * l_i + jnp.sum(p, axis=-1, keepdims=True)
        acc = alpha * acc + jnp.dot(p.astype(BF16), v, preferred_element_type=F32)
        return m_new, l_new, acc

    def full_block(kb, carry):
        s, v = scores(kb)
        return update(carry, s, v)

    init = (jnp.full((bq, 1), NEG_INF, F32), jnp.zeros((bq, 1), F32),
            jnp.zeros((bq, MLA_V), F32))
    carry = lax.fori_loop(0, qi * steps, full_block, init)
    rq = lax.broadcasted_iota(jnp.int32, (bq, bk), 0)
    ck = lax.broadcasted_iota(jnp.int32, (bq, bk), 1)
    for t in range(steps):
        s, v = scores(qi * steps + t)
        s = jnp.where(ck + t * bk <= rq, s, NEG_INF)
        carry = update(carry, s, v)
    _, l_i, acc = carry
    o_ref[0] = (acc / l_i).astype(o_ref.dtype)


def mla_attention(q, kv, c, *, bsz, seq, kpe_block):
    hq = MLA_NOPE + V7X_LANES
    heads = q.shape[1] // hq
    bq = _tile(seq, MLA_BQ)
    bk = _tile(bq, MLA_BK)
    q3 = q.reshape(bsz, seq, heads * hq)
    kv3 = kv.reshape(bsz, seq, kv.shape[1])
    c3 = c.reshape(bsz, seq, c.shape[1])
    o = pl.pallas_call(
        functools.partial(_mla_attn_kernel, bk=bk),
        out_shape=jax.ShapeDtypeStruct((bsz, seq, heads * MLA_V), BF16),
        grid=(bsz, heads, seq // bq),
        in_specs=[pl.BlockSpec((1, bq, hq), lambda b, h, i: (b, i, h)),
                  pl.BlockSpec((1, seq, MLA_NOPE + MLA_V), lambda b, h, i: (b, 0, h)),
                  pl.BlockSpec((1, seq, V7X_LANES), lambda b, h, i: (b, 0, kpe_block))],
        out_specs=pl.BlockSpec((1, bq, MLA_V), lambda b, h, i: (b, i, h)),
        compiler_params=_params("parallel", "parallel", "parallel"),
        name="mla_attn",
    )(q3, kv3, c3)
    return o.reshape(bsz * seq, heads * MLA_V)


def _sgu_out_kernel(u_ref, v_ref, st_ref, lng_ref, lnb_ref, ws_ref, bs_ref, wo_ref, bo_ref,
                    h_ref, o_ref, acc_ref, gate_ref, *, width):
    g = pl.program_id(1)
    ng = pl.num_programs(1)
    bm = u_ref.shape[0]
    ch = ws_ref.shape[1]
    st = st_ref[...]
    mu = st[:, 0:1] * (1.0 / width)
    var = st[:, 1:2] * (1.0 / width) - mu * mu
    rstd = lax.rsqrt(var + NORM_EPS)
    ti = lax.broadcasted_iota(jnp.int32, (ch, ch), 0)
    si = lax.broadcasted_iota(jnp.int32, (ch, ch), 1)
    ws = jnp.where(si <= ti, ws_ref[0], 0.0).astype(BF16)
    bs = bs_ref[0]
    lng = lng_ref[...]
    lnb = lnb_ref[...]
    for c in range(bm // ch):
        rows = slice(c * ch, (c + 1) * ch)
        vn = (v_ref[rows, :].astype(F32) - mu[rows]) * rstd[rows] * lng + lnb
        mixed = jnp.dot(ws, vn.astype(BF16), preferred_element_type=F32) + bs
        gate_ref[rows, :] = (u_ref[rows, :].astype(F32) * mixed).astype(BF16)
    part = jnp.dot(gate_ref[...], wo_ref[...], preferred_element_type=F32)

    @pl.when(g == 0)
    def _():
        acc_ref[...] = part

    @pl.when(g > 0)
    def _():
        acc_ref[...] += part

    @pl.when(g == ng - 1)
    def _():
        o_ref[...] = acc_ref[...] + bo_ref[...] + h_ref[...]


def sgu_out(u, v, st, ln_g, ln_b, w_s, b_s, w_out, b_out, h):
    m, e = u.shape
    d = w_out.shape[1]
    groups, ch, _ = w_s.shape
    gw = e // groups
    bm = _tile(m, SGU_BM)
    assert bm % ch == 0
    return pl.pallas_call(
        functools.partial(_sgu_out_kernel, width=e),
        out_shape=jax.ShapeDtypeStruct((m, d), F32),
        grid=(m // bm, groups),
        in_specs=[
            pl.BlockSpec((bm, gw), lambda i, g: (i, g)),
            pl.BlockSpec((bm, gw), lambda i, g: (i, g)),
            pl.BlockSpec((bm, V7X_LANES), lambda i, g: (i, 0)),
            pl.BlockSpec((1, gw), lambda i, g: (0, g)),
            pl.BlockSpec((1, gw), lambda i, g: (0, g)),
            pl.BlockSpec((1, ch, ch), lambda i, g: (g, 0, 0)),
            pl.BlockSpec((1, ch, 1), lambda i, g: (g, 0, 0)),
            pl.BlockSpec((gw, d), lambda i, g: (g, 0)),
            pl.BlockSpec((1, d), lambda i, g: (0, 0)),
            pl.BlockSpec((bm, d), lambda i, g: (i, 0)),
        ],
        out_specs=pl.BlockSpec((bm, d), lambda i, g: (i, 0)),
        scratch_shapes=[pltpu.VMEM((bm, d), F32), pltpu.VMEM((bm, gw), BF16)],
        compiler_params=_params("parallel", "arbitrary"),
        name="sgu_out",
    )(u, v, st, ln_g.reshape(1, e).astype(F32), ln_b.reshape(1, e).astype(F32),
      w_s.astype(F32), b_s.reshape(groups, ch, 1).astype(F32), w_out.astype(BF16),
      b_out.reshape(1, d).astype(F32), h)


def _rope_tables(seq, dim):
    half = dim // 2
    pos = jnp.arange(seq, dtype=F32)
    inv = ROPE_THETA ** (-(jnp.arange(0, dim, 2, dtype=F32) / dim))
    ang = pos[:, None] * inv[None, :]
    zeros = jnp.zeros((seq, V7X_LANES // 2 - half), F32)
    cos = jnp.concatenate([jnp.cos(ang), zeros, jnp.cos(ang), zeros], axis=1)
    sin = jnp.concatenate([-jnp.sin(ang), zeros, jnp.sin(ang), zeros], axis=1)
    return cos, sin


def _spread_rope_cols(w):
    k, dim = w.shape
    half = dim // 2
    zeros = jnp.zeros((k, V7X_LANES // 2 - half), w.dtype)
    return jnp.concatenate([w[:, :half], zeros, w[:, half:], zeros], axis=1)


def conformer_mixer(xn, h, w_in, b_in, dw_w, dw_b, ln_g, ln_b, w_out, b_out, *, bsz, seq):
    c = w_out.shape[0]
    wi = w_in.astype(BF16)
    z = matmul(xn, wi[:, :c], w2=wi[:, c:], bias=b_in[:c], bias2=b_in[c:])
    z = causal_dwconv(z.reshape(bsz, seq, c), dw_w, dw_b).reshape(bsz * seq, c)
    return matmul(z, w_out.astype(BF16), prologue="ln_silu", pg=ln_g, pb=ln_b, bias=b_out,
                  residual=h, out_dtype=F32)


def dilated_attention_mixer(xn, h, w_qkv, w_o, *, bsz, seq):
    d = xn.shape[1]
    dh = DSA_HEAD_DIM
    heads = w_o.shape[0] // dh
    groups = len(DSA_CONFIGS)
    hw = heads * dh
    w5 = w_qkv.reshape(d, groups, 3, hw)
    scale = jnp.array([dh ** -0.5, 1.0, 1.0], F32).reshape(1, 1, 3, 1)
    w5 = (w5 * scale).astype(BF16)
    tables = _rope_tables(seq, dh)
    outs, lses = [], []
    for g, (window, dil) in enumerate(DSA_CONFIGS):
        qkv = dsa_project(xn, w5[:, g].reshape(d, 3 * hw), tables, dil, bsz=bsz, seq=seq,
                          rope_cols=2 * hw)
        o, lse = band_attention(qkv, window, dil, bsz=bsz, seq=seq, heads=heads)
        outs.append(o)
        lses.append(lse)
    o = merge_groups(outs, lses, [dil for _, dil in DSA_CONFIGS], heads=heads)
    return matmul(o, w_o.astype(BF16), residual=h, out_dtype=F32)


def mla_mixer(xn, h, w_in, q_norm, w_qb, kv_norm, w_kvb, w_o, *, bsz, seq):
    q_rank = q_norm.shape[0]
    kv_rank = kv_norm.shape[0]
    assert q_rank == kv_rank
    heads = MLA_HEADS
    w_in_p = jnp.concatenate(
        [w_in[:, :q_rank + kv_rank], _spread_rope_cols(w_in[:, q_rank + kv_rank:])],
        axis=1).astype(BF16)
    n_c = w_in_p.shape[1]
    kpe_group = (q_rank + kv_rank) // V7X_LANES
    cmask = tuple(k == kpe_group for k in range(n_c // V7X_LANES))
    tables = _rope_tables(seq, MLA_ROPE)
    c = matmul(xn, w_in_p, rope=tables, rope_mask=cmask, seq=seq, out_dtype=F32, bn=n_c)
    wq3 = w_qb.reshape(q_rank, heads, MLA_NOPE + MLA_ROPE) * ((MLA_NOPE + MLA_ROPE) ** -0.5)
    wq_pe = jax.vmap(_spread_rope_cols, in_axes=1, out_axes=1)(wq3[:, :, MLA_NOPE:])
    wq_p = jnp.concatenate([wq3[:, :, :MLA_NOPE], wq_pe], axis=2)
    wq_p = wq_p.reshape(q_rank, heads * (MLA_NOPE + V7X_LANES)).astype(BF16)
    bn = _tile(wq_p.shape[1], 1024)
    qmask = tuple(k % 2 == 1 for k in range(bn // V7X_LANES))
    q = matmul(c, wq_p, x_cols=(0, q_rank), prologue="rms", pg=q_norm, rope=tables,
               rope_mask=qmask, seq=seq, bn=bn)
    kv = matmul(c, w_kvb.astype(BF16), x_cols=(1, kv_rank), prologue="rms", pg=kv_norm)
    o = mla_attention(q, kv, c, bsz=bsz, seq=seq, kpe_block=kpe_group)
    return matmul(o, w_o.astype(BF16), residual=h, out_dtype=F32)


def sgu_mixer(xn, h, w_in, b_in, ln_g, ln_b, w_s, b_s, w_out, b_out):
    e = w_out.shape[0]
    wi = w_in.astype(BF16)
    u = matmul(xn, wi[:, :e], bias=b_in[:e], gelu=True)
    v, st = matmul(xn, wi[:, e:], bias=b_in[e:], gelu=True, stats=True)
    return sgu_out(u, v, st, ln_g, ln_b, w_s, b_s, w_out, b_out, h)


def kernel(x, l0_norm_mix, l0_cc_w_in, l0_cc_b_in, l0_cc_dw_w, l0_cc_dw_b, l0_cc_ln_g, l0_cc_ln_b, l0_cc_w_out, l0_cc_b_out, l0_norm_ffn, l0_ffn_w_up, l0_ffn_dw_w, l0_ffn_dw_b, l0_ffn_w_down, l1_norm_mix, l1_dsa_w_qkv, l1_dsa_w_o, l1_norm_ffn, l1_ffn_w_up, l1_ffn_dw_w, l1_ffn_dw_b, l1_ffn_w_down, l2_norm_mix, l2_mla_w_in, l2_mla_q_norm, l2_mla_w_qb, l2_mla_kv_norm, l2_mla_w_kvb, l2_mla_w_o, l2_norm_ffn, l2_ffn_w_up, l2_ffn_dw_w, l2_ffn_dw_b, l2_ffn_w_down, l3_norm_mix, l3_sg_w_in, l3_sg_b_in, l3_sg_ln_g, l3_sg_ln_b, l3_sg_w_s, l3_sg_b_s, l3_sg_w_out, l3_sg_b_out, l3_norm_ffn, l3_ffn_w_up, l3_ffn_dw_w, l3_ffn_dw_b, l3_ffn_w_down, final_norm):
    bsz, seq, d = x.shape
    dims = dict(bsz=bsz, seq=seq)
    h = x.reshape(bsz * seq, d)

    def ffn(h, norm, w_up, dw_w, dw_b, w_down):
        return conv_ffn(h, norm, w_up, dw_w, dw_b, w_down, seq=seq)

    h = conformer_mixer(rmsnorm(h, l0_norm_mix, BF16), h, l0_cc_w_in, l0_cc_b_in, l0_cc_dw_w,
                        l0_cc_dw_b, l0_cc_ln_g, l0_cc_ln_b, l0_cc_w_out, l0_cc_b_out, **dims)
    h = ffn(h, l0_norm_ffn, l0_ffn_w_up, l0_ffn_dw_w, l0_ffn_dw_b, l0_ffn_w_down)
    h = dilated_attention_mixer(rmsnorm(h, l1_norm_mix, BF16), h, l1_dsa_w_qkv, l1_dsa_w_o,
                                **dims)
    h = ffn(h, l1_norm_ffn, l1_ffn_w_up, l1_ffn_dw_w, l1_ffn_dw_b, l1_ffn_w_down)
    h = mla_mixer(rmsnorm(h, l2_norm_mix, BF16), h, l2_mla_w_in, l2_mla_q_norm, l2_mla_w_qb,
                  l2_mla_kv_norm, l2_mla_w_kvb, l2_mla_w_o, **dims)
    h = ffn(h, l2_norm_ffn, l2_ffn_w_up, l2_ffn_dw_w, l2_ffn_dw_b, l2_ffn_w_down)
    h = sgu_mixer(rmsnorm(h, l3_norm_mix, BF16), h, l3_sg_w_in, l3_sg_b_in, l3_sg_ln_g,
                  l3_sg_ln_b, l3_sg_w_s, l3_sg_b_s, l3_sg_w_out, l3_sg_b_out)
    h = ffn(h, l3_norm_ffn, l3_ffn_w_up, l3_ffn_dw_w, l3_ffn_dw_b, l3_ffn_w_down)
    return rmsnorm(h, final_norm, F32).reshape(bsz, seq, d)
```

```python
import functools
import math

import jax
import jax.numpy as jnp
from jax import lax
from jax.experimental import pallas as pl
from jax.experimental.pallas import tpu as pltpu

F32 = jnp.float32
BF16 = jnp.bfloat16

NORM_EPS = 1e-6
ROPE_THETA = 10000.0
NEG_INF = -1e30

DSA_HEAD_DIM = 128
DSA_CONFIGS = ((128, 1), (512, 4), (2048, 16))
DSA_BLOCK = 128
MLA_HEADS = 16
MLA_NOPE = 128
MLA_ROPE = 64
MLA_V = 128
SG_CHUNK = 128
SG_GROUPS = 8

V7X_LANES = 128
V7X_SUBLANES = 8
V7X_VMEM_LIMIT_BYTES = 56 * 1024 * 1024

MM_BM, MM_BN = 1024, 1024
MM_PROLOGUE_BM = 512
FFN_BM, FFN_FC = 1024, 512
FFN_SUB, FFN_RSUB = 256, 256
MLA_BQ, MLA_BK = 512, 512
SGU_BM = 512
CONV_CB, CONV_ROWS = 256, 64
DSA_PROJ_BN = 512
MLA_PROJ_BN = 2048


def _params(*semantics):
    return pltpu.CompilerParams(dimension_semantics=semantics,
                                vmem_limit_bytes=V7X_VMEM_LIMIT_BYTES)


def _sigmoid(x):
    return 1.0 / (1.0 + jnp.exp(-x))


def _tile(n, want):
    t = min(n, want)
    while n % t:
        t -= 1
    return t


def _rmsnorm_kernel(x_ref, g_ref, o_ref):
    x = x_ref[...].astype(F32)
    y = x * lax.rsqrt(jnp.mean(x * x, axis=-1, keepdims=True) + NORM_EPS)
    o_ref[...] = (y * g_ref[...]).astype(o_ref.dtype)


def rmsnorm(x, g, out_dtype):
    m, d = x.shape
    bm = _tile(m, 512)
    return pl.pallas_call(
        _rmsnorm_kernel,
        out_shape=jax.ShapeDtypeStruct((m, d), out_dtype),
        grid=(m // bm,),
        in_specs=[pl.BlockSpec((bm, d), lambda i: (i, 0)),
                  pl.BlockSpec((1, d), lambda i: (0, 0))],
        out_specs=pl.BlockSpec((bm, d), lambda i: (i, 0)),
        compiler_params=_params("parallel"),
        name="rmsnorm",
    )(x, g.reshape(1, d).astype(F32))


def _mm_kernel(*refs, prologue, has_bias, glu, gelu, rope_mask, residual, stats):
    it = iter(refs)
    x_ref = next(it)
    pg_ref = next(it) if prologue else None
    pb_ref = next(it) if prologue == "ln_silu" else None
    w_ref = next(it)
    w2_ref = next(it) if glu else None
    b_ref = next(it) if has_bias else None
    b2_ref = next(it) if (glu and has_bias) else None
    cos_ref = next(it) if rope_mask else None
    sin_ref = next(it) if rope_mask else None
    res_ref = next(it) if residual else None
    o_ref = next(it)
    st_ref = next(it) if stats else None
    xs_ref = next(it) if prologue else None

    j = pl.program_id(1)

    if prologue:
        @pl.when(j == 0)
        def _():
            xf = x_ref[...].astype(F32)
            if prologue == "rms":
                y = xf * lax.rsqrt(jnp.mean(xf * xf, axis=-1, keepdims=True) + NORM_EPS)
                y = y * pg_ref[...]
            else:
                mu = jnp.mean(xf, axis=-1, keepdims=True)
                xc = xf - mu
                y = xc * lax.rsqrt(jnp.mean(xc * xc, axis=-1, keepdims=True) + NORM_EPS)
                y = y * pg_ref[...] + pb_ref[...]
                y = y * _sigmoid(y)
            xs_ref[...] = y.astype(BF16)
        xv = xs_ref[...]
    else:
        xv = x_ref[...]

    acc = jnp.dot(xv, w_ref[...], preferred_element_type=F32)
    if has_bias:
        acc = acc + b_ref[...]
    if glu:
        gate = jnp.dot(xv, w2_ref[...], preferred_element_type=F32)
        if has_bias:
            gate = gate + b2_ref[...]
        acc = acc * _sigmoid(gate)
    if gelu:
        acc = 0.5 * acc * (1.0 + lax.erf(acc * (2.0 ** -0.5)))
    if stats:
        s1 = jnp.sum(acc, axis=-1, keepdims=True)
        s2 = jnp.sum(acc * acc, axis=-1, keepdims=True)
        lane = lax.broadcasted_iota(jnp.int32, st_ref.shape, 1)
        upd = jnp.where(lane == 0, s1, jnp.where(lane == 1, s2, 0.0))

        @pl.when(j == 0)
        def _():
            st_ref[...] = upd

        @pl.when(j > 0)
        def _():
            st_ref[...] += upd
    if residual:
        acc = acc + res_ref[...]
    if rope_mask:
        cos = cos_ref[...]
        sin = sin_ref[...]
        for k, on in enumerate(rope_mask):
            sl = acc[:, k * V7X_LANES:(k + 1) * V7X_LANES]
            if on:
                sl = sl * cos + pltpu.roll(sl, V7X_LANES // 2, 1) * sin
            o_ref[:, k * V7X_LANES:(k + 1) * V7X_LANES] = sl.astype(o_ref.dtype)
    else:
        o_ref[...] = acc.astype(o_ref.dtype)


def matmul(x, w, *, x_cols=None, prologue=None, pg=None, pb=None, w2=None, bias=None,
           bias2=None, gelu=False, rope=None, rope_mask=None, seq=None, residual=None,
           stats=False, out_dtype=BF16, bm=None, bn=None):
    m = x.shape[0]
    k, n = w.shape
    kblk = 0 if x_cols is None else x_cols[0]
    bm = _tile(m, bm or (MM_PROLOGUE_BM if prologue else MM_BM))
    bn = _tile(n, bn or MM_BN)
    glu = w2 is not None
    has_bias = bias is not None
    if rope_mask:
        assert bn % V7X_LANES == 0 and len(rope_mask) == bn // V7X_LANES
        assert seq % bm == 0
    args, specs = [x], [pl.BlockSpec((bm, k), lambda i, j: (i, kblk))]
    if prologue:
        args.append(pg.reshape(1, k).astype(F32))
        specs.append(pl.BlockSpec((1, k), lambda i, j: (0, 0)))
        if prologue == "ln_silu":
            args.append(pb.reshape(1, k).astype(F32))
            specs.append(pl.BlockSpec((1, k), lambda i, j: (0, 0)))
    args.append(w)
    specs.append(pl.BlockSpec((k, bn), lambda i, j: (0, j)))
    if glu:
        args.append(w2)
        specs.append(pl.BlockSpec((k, bn), lambda i, j: (0, j)))
    if has_bias:
        args.append(bias.reshape(1, n).astype(F32))
        specs.append(pl.BlockSpec((1, bn), lambda i, j: (0, j)))
        if glu:
            args.append(bias2.reshape(1, n).astype(F32))
            specs.append(pl.BlockSpec((1, bn), lambda i, j: (0, j)))
    if rope_mask:
        tiles_per_seq = seq // bm
        for t in rope:
            args.append(t)
            specs.append(pl.BlockSpec((bm, V7X_LANES), lambda i, j: (i % tiles_per_seq, 0)))
    if residual is not None:
        args.append(residual)
        specs.append(pl.BlockSpec((bm, bn), lambda i, j: (i, j)))
    out_shape = [jax.ShapeDtypeStruct((m, n), out_dtype)]
    out_specs = [pl.BlockSpec((bm, bn), lambda i, j: (i, j))]
    if stats:
        out_shape.append(jax.ShapeDtypeStruct((m, V7X_LANES), F32))
        out_specs.append(pl.BlockSpec((bm, V7X_LANES), lambda i, j: (i, 0)))
    scratch = [pltpu.VMEM((bm, k), BF16)] if prologue else []
    kern = functools.partial(
        _mm_kernel, prologue=prologue, has_bias=has_bias, glu=glu, gelu=gelu,
        rope_mask=tuple(rope_mask) if rope_mask else None,
        residual=residual is not None, stats=stats)
    out = pl.pallas_call(
        kern,
        out_shape=out_shape,
        grid=(m // bm, n // bn),
        in_specs=specs,
        out_specs=out_specs,
        scratch_shapes=scratch,
        compiler_params=_params("parallel", "arbitrary"),
        name="matmul",
    )(*args)
    return out if stats else out[0]


def _dwconv_kernel(x_ref, w_ref, b_ref, o_ref, sh_ref, *, width, row_chunk):
    s, cb = x_ref.shape[1], x_ref.shape[2]
    pad = sh_ref.shape[1] - s
    x = x_ref[0].astype(F32)
    xz = jnp.concatenate([x, jnp.zeros((V7X_SUBLANES, cb), F32)], axis=0)
    zero_top = jnp.zeros((pad, cb), F32)
    for r in range(V7X_SUBLANES):
        sh_ref[r, :pad, :] = zero_top
        shifted = xz if r == 0 else pltpu.roll(xz, r, 0)
        sh_ref[r, pad:, :] = shifted[:s]

    w = w_ref[...]
    bias = b_ref[...]

    def chunk(c, carry):
        base = pl.multiple_of(c * row_chunk, row_chunk)
        acc = jnp.zeros((row_chunk, cb), F32) + bias
        for kk in range(width):
            shift = width - 1 - kk
            q, r = divmod(shift, V7X_SUBLANES)
            start = pl.multiple_of(base + (pad - q * V7X_SUBLANES), V7X_SUBLANES)
            tap = sh_ref[r, pl.ds(start, row_chunk), :]
            acc = acc + tap * w[kk:kk + 1, :]
        o_ref[0, pl.ds(base, row_chunk), :] = acc.astype(o_ref.dtype)
        return carry

    lax.fori_loop(0, s // row_chunk, chunk, 0)


def causal_dwconv(x, w, b):
    bsz, s, c = x.shape
    width = w.shape[0]
    cb = _tile(c, CONV_CB)
    row_chunk = _tile(s, CONV_ROWS)
    pad = -(-(width - 1) // V7X_SUBLANES) * V7X_SUBLANES
    return pl.pallas_call(
        functools.partial(_dwconv_kernel, width=width, row_chunk=row_chunk),
        out_shape=jax.ShapeDtypeStruct((bsz, s, c), BF16),
        grid=(bsz, c // cb),
        in_specs=[pl.BlockSpec((1, s, cb), lambda i, j: (i, 0, j)),
                  pl.BlockSpec((width, cb), lambda i, j: (0, j)),
                  pl.BlockSpec((1, cb), lambda i, j: (0, j))],
        out_specs=pl.BlockSpec((1, s, cb), lambda i, j: (i, 0, j)),
        scratch_shapes=[pltpu.VMEM((V7X_SUBLANES, pad + s, cb), F32)],
        compiler_params=_params("parallel", "parallel"),
        name="dwconv",
    )(x, w.astype(F32), b.reshape(1, c).astype(F32))


def _ffn_kernel(h_ref, ng_ref, og_ref, wg_ref, wa_ref, cwg_ref, cwa_ref, cbg_ref, cba_ref,
                wd_ref, o_ref, xs_ref, acc_ref, carry_g_ref, carry_a_ref, *, tiles_per_seq, sub,
                rsub, norm_out):
    i = pl.program_id(0)
    j = pl.program_id(1)
    nj = pl.num_programs(1)
    bm, d = h_ref.shape
    fc = wg_ref.shape[1]
    seq_start = (i % tiles_per_seq) == 0
    sl = V7X_SUBLANES
    nv = rsub // sl
    slabs = [slice(c * V7X_LANES, (c + 1) * V7X_LANES) for c in range(d // V7X_LANES)]

    na = nv // sl

    def staged(r, a, k):
        return pl.ds(r * rsub + sl * sl * a + k, sl, stride=sl)

    @pl.when(seq_start)
    def _():
        carry_g_ref[j] = jnp.zeros((2 * sl, fc), F32)
        carry_a_ref[j] = jnp.zeros((2 * sl, fc), F32)

    @pl.when(j == 0)
    def _():
        gain = ng_ref[...]
        for r in range(bm // rsub):
            for s in range(sl):
                for a in range(na):
                    src = r * rsub + nv * s + sl * a
                    dst = r * rsub + sl * sl * a + sl * s
                    hv = h_ref[src:src + sl, :]
                    y = hv * lax.rsqrt(jnp.mean(hv * hv, axis=-1, keepdims=True) + NORM_EPS) * gain
                    for c, cols in enumerate(slabs):
                        acc_ref[c, dst:dst + sl, :] = y[:, cols]
        for r in range(bm // rsub):
            for a in range(na):
                for b in range(0, sl, 2):
                    dst = r * rsub + sl * (sl * a + b)
                    for c, cols in enumerate(slabs):
                        pair = jnp.concatenate([acc_ref[c, staged(r, a, b), :],
                                                acc_ref[c, staged(r, a, b + 1), :]], axis=0)
                        xs_ref[dst:dst + 2 * sl, cols] = pair.astype(BF16)
        acc_ref[...] = jnp.zeros(acc_ref.shape, F32)

    first_row = lax.broadcasted_iota(jnp.int32, (sl, sub), 0) == 0

    def shift1(z, prev_row):
        top = jnp.where(first_row, prev_row, pltpu.roll(z[rsub - sl:, :], 1, 0))
        return jnp.concatenate([top, z[:rsub - sl, :]], axis=0)

    def conv(z, prev, cw, cb):
        z1 = shift1(z, prev[2 * sl - 1:2 * sl, :])
        z2 = shift1(z1, prev[sl - 1:sl, :])
        return z2 * cw[0:1, :] + z1 * cw[1:2, :] + z * cw[2:3, :] + cb

    tiles = [(slice(r * rsub, (r + 1) * rsub), slice(c * sub, (c + 1) * sub))
             for c in range(fc // sub) for r in range(bm // rsub)]

    def up(rows, cols):
        xr = xs_ref[rows, :]
        return (jnp.dot(xr, wg_ref[:, cols], preferred_element_type=F32),
                jnp.dot(xr, wa_ref[:, cols], preferred_element_type=F32))

    z_next = up(*tiles[0])
    for k, (rows, cols) in enumerate(tiles):
        zg, za = z_next
        if k + 1 < len(tiles):
            z_next = up(*tiles[k + 1])
        if rows.start == 0:
            prev_g = carry_g_ref[j, :, cols]
            prev_a = carry_a_ref[j, :, cols]
        g = conv(zg, prev_g, cwg_ref[:, cols], cbg_ref[:, cols])
        a = conv(za, prev_a, cwa_ref[:, cols], cba_ref[:, cols])
        prev_g = zg[rsub - 2 * sl:, :]
        prev_a = za[rsub - 2 * sl:, :]
        if rows.stop == bm:
            carry_g_ref[j, :, cols] = prev_g
            carry_a_ref[j, :, cols] = prev_a
        hg = 0.5 * g
        act = ((hg + hg * jnp.tanh(hg)) * a).astype(BF16)
        part = jnp.dot(act, wd_ref[cols, :], preferred_element_type=F32)
        for c, dcols in enumerate(slabs):
            acc_ref[c, rows, :] += part[:, dcols]

    @pl.when(j == nj - 1)
    def _():
        for r in range(bm // rsub):
            for s in range(sl):
                for a in range(na):
                    dst = r * rsub + nv * s + sl * a
                    vals = [acc_ref[c, staged(r, a, s), :] + h_ref[dst:dst + sl, cols]
                            for c, cols in enumerate(slabs)]
                    if norm_out:
                        ss = functools.reduce(
                            lambda p, q: p + q,
                            [jnp.sum(v * v, axis=-1, keepdims=True) for v in vals])
                        inv = lax.rsqrt(ss * (1.0 / d) + NORM_EPS)
                        vals = [v * inv * og_ref[:, cols] for v, cols in zip(vals, slabs)]
                    for v, cols in zip(vals, slabs):
                        o_ref[dst:dst + sl, cols] = v


def conv_ffn(h, norm_g, w_up, dw_w, dw_b, w_down, *, seq, out_g=None):
    m, d = h.shape
    f = w_down.shape[0]
    bm = _tile(seq, FFN_BM)
    fc = _tile(f, FFN_FC)
    nf = f // fc
    wu = w_up.astype(BF16).reshape(d, 2 * nf, fc).transpose(1, 0, 2)
    wd = w_down.astype(BF16)
    cw = dw_w.astype(F32)
    cb = dw_b.reshape(1, 2 * f).astype(F32)
    rsub = _tile(bm, FFN_RSUB)
    assert rsub % (V7X_SUBLANES * V7X_SUBLANES) == 0 and d % V7X_LANES == 0
    kern = functools.partial(_ffn_kernel, tiles_per_seq=seq // bm, sub=_tile(fc, FFN_SUB),
                             rsub=rsub, norm_out=out_g is not None)
    gains = [g.reshape(1, d).astype(F32) for g in (norm_g, norm_g if out_g is None else out_g)]
    return pl.pallas_call(
        kern,
        out_shape=jax.ShapeDtypeStruct((m, d), F32),
        grid=(m // bm, nf),
        in_specs=[
            pl.BlockSpec((bm, d), lambda i, j: (i, 0), pipeline_mode=pl.Buffered(1)),
            pl.BlockSpec((1, d), lambda i, j: (0, 0)),
            pl.BlockSpec((1, d), lambda i, j: (0, 0)),
            pl.BlockSpec((None, d, fc), lambda i, j: (j, 0, 0)),
            pl.BlockSpec((None, d, fc), lambda i, j: (j + nf, 0, 0)),
            pl.BlockSpec((3, fc), lambda i, j: (0, j)),
            pl.BlockSpec((3, fc), lambda i, j: (0, j + nf)),
            pl.BlockSpec((1, fc), lambda i, j: (0, j)),
            pl.BlockSpec((1, fc), lambda i, j: (0, j + nf)),
            pl.BlockSpec((fc, d), lambda i, j: (j, 0)),
        ],
        out_specs=pl.BlockSpec((bm, d), lambda i, j: (i, 0)),
        scratch_shapes=[
            pltpu.VMEM((bm, d), BF16),
            pltpu.VMEM((d // V7X_LANES, bm, V7X_LANES), F32),
            pltpu.VMEM((nf, 2 * V7X_SUBLANES, fc), F32),
            pltpu.VMEM((nf, 2 * V7X_SUBLANES, fc), F32),
        ],
        compiler_params=_params("arbitrary", "arbitrary"),
        name="conv_ffn",
    )(h, *gains, wu, wu, cw, cw, cb, cb, wd)


def _band_attn_kernel(q_ref, kp_ref, kc_ref, vp_ref, vc_ref, o_ref, lse_ref, *, heads, span):
    n = pl.program_id(2)
    blk = q_ref.shape[1]
    qi = lax.broadcasted_iota(jnp.int32, (blk, 2 * blk), 0)
    kj = lax.broadcasted_iota(jnp.int32, (blk, 2 * blk), 1)
    dist = blk + qi - kj
    first_key = jnp.where(n > 0, 0, blk)
    mask = (dist >= 0) & (dist <= span) & (kj >= first_key)
    lane = lax.broadcasted_iota(jnp.int32, (blk, V7X_LANES), 1)
    lse_tile = jnp.zeros((blk, V7X_LANES), F32)
    dh = DSA_HEAD_DIM
    for hh in range(heads):
        cols = slice(hh * dh, (hh + 1) * dh)
        q = q_ref[0, :, cols]
        k = jnp.concatenate([kp_ref[0, :, cols], kc_ref[0, :, cols]], axis=0)
        v = jnp.concatenate([vp_ref[0, :, cols], vc_ref[0, :, cols]], axis=0)
        s = lax.dot_general(q, k, (((1,), (1,)), ((), ())), preferred_element_type=F32)
        s = jnp.where(mask, s, NEG_INF)
        mx = jnp.max(s, axis=-1, keepdims=True)
        p = jnp.exp(s - mx)
        l = jnp.sum(p, axis=-1, keepdims=True)
        o = jnp.dot(p.astype(BF16), v, preferred_element_type=F32) / l
        o_ref[0, :, cols] = o.astype(o_ref.dtype)
        lse_tile = jnp.where(lane == hh, mx + jnp.log(l), lse_tile)
    lse_ref[0] = lse_tile


def band_attention(qkv, window, dil, *, bsz, seq, heads):
    hw = heads * DSA_HEAD_DIM
    blk = DSA_BLOCK
    nb = seq // dil // blk
    span = window // dil
    qkv3 = qkv.reshape(bsz, seq, 3 * hw)
    cur = lambda r, n: r * nb + n
    prev = lambda r, n: r * nb + jnp.maximum(n - 1, 0)
    return pl.pallas_call(
        functools.partial(_band_attn_kernel, heads=heads, span=span),
        out_shape=[jax.ShapeDtypeStruct((bsz, seq, hw), BF16),
                   jax.ShapeDtypeStruct((bsz, seq, V7X_LANES), F32)],
        grid=(bsz, dil, nb),
        in_specs=[
            pl.BlockSpec((1, blk, hw), lambda b, r, n: (b, cur(r, n), 0)),
            pl.BlockSpec((1, blk, hw), lambda b, r, n: (b, prev(r, n), 1)),
            pl.BlockSpec((1, blk, hw), lambda b, r, n: (b, cur(r, n), 1)),
            pl.BlockSpec((1, blk, hw), lambda b, r, n: (b, prev(r, n), 2)),
            pl.BlockSpec((1, blk, hw), lambda b, r, n: (b, cur(r, n), 2)),
        ],
        out_specs=[pl.BlockSpec((1, blk, hw), lambda b, r, n: (b, cur(r, n), 0)),
                   pl.BlockSpec((1, blk, V7X_LANES), lambda b, r, n: (b, cur(r, n), 0))],
        compiler_params=_params("parallel", "parallel", "parallel"),
        name="band_attn",
    )(qkv3, qkv3, qkv3, qkv3, qkv3)


def _dsa_proj_kernel(x_ref, w_ref, cos_ref, sin_ref, o_ref, *scratch, dil, rope_tiles):
    j = pl.program_id(1)
    s = x_ref.shape[0]
    ln = s // dil
    acc = jnp.dot(x_ref[...], w_ref[...], preferred_element_type=F32)
    slabs = [slice(c * V7X_LANES, (c + 1) * V7X_LANES) for c in range(acc.shape[1] // V7X_LANES)]
    if dil > 1:
        stage_ref, = scratch
        for c, cols in enumerate(slabs):
            stage_ref[c] = acc[:, cols]

    def emit(rope):
        for r in range(dil):
            rows = slice(r * ln, (r + 1) * ln)
            for c, cols in enumerate(slabs):
                blk = stage_ref[c, pl.ds(r, ln, stride=dil), :] if dil > 1 else acc[:, cols]
                if rope:
                    blk = (blk * cos_ref[rows, :]
                           + pltpu.roll(blk, V7X_LANES // 2, 1) * sin_ref[rows, :])
                o_ref[rows, cols] = blk.astype(o_ref.dtype)

    @pl.when(j < rope_tiles)
    def _():
        emit(True)

    @pl.when(j >= rope_tiles)
    def _():
        emit(False)


def dsa_project(xn, w_g, tables, dil, *, bsz, seq, rope_cols):
    m, d = xn.shape
    n = w_g.shape[1]
    bn = _tile(n, DSA_PROJ_BN)
    assert rope_cols % bn == 0
    residue_major = lambda t: t.reshape(seq // dil, dil, V7X_LANES).transpose(1, 0, 2).reshape(
        seq, V7X_LANES)
    cos, sin = (residue_major(t) for t in tables)
    scratch = [pltpu.VMEM((bn // V7X_LANES, seq, V7X_LANES), F32)] if dil > 1 else []
    return pl.pallas_call(
        functools.partial(_dsa_proj_kernel, dil=dil, rope_tiles=rope_cols // bn),
        out_shape=jax.ShapeDtypeStruct((m, n), BF16),
        grid=(bsz, n // bn),
        in_specs=[pl.BlockSpec((seq, d), lambda b, j: (b, 0)),
                  pl.BlockSpec((d, bn), lambda b, j: (0, j)),
                  pl.BlockSpec((seq, V7X_LANES), lambda b, j: (0, 0)),
                  pl.BlockSpec((seq, V7X_LANES), lambda b, j: (0, 0))],
        out_specs=pl.BlockSpec((seq, bn), lambda b, j: (b, j)),
        scratch_shapes=scratch,
        compiler_params=_params("parallel", "arbitrary"),
        name="dsa_proj",
    )(xn, w_g, cos, sin)


def _merge_kernel(*refs, dils, heads):
    groups = len(dils)
    o_refs = refs[:groups]
    lse_refs = refs[groups:2 * groups]
    out_ref, so_ref, sl_ref = refs[2 * groups:]
    s = out_ref.shape[1]

    def to_sequence_order(dst_ref, g, rows_of):
        ln = s // dils[g]
        for r in range(dils[g]):
            dst_ref[g, pl.ds(r, ln, stride=dils[g]), :] = rows_of(slice(r * ln, (r + 1) * ln))

    for g in range(groups):
        to_sequence_order(sl_ref, g, lambda rows, g=g: lse_refs[g][0, rows, :])
    lses = [sl_ref[g] for g in range(groups)]
    mx = functools.reduce(jnp.maximum, lses)
    es = [jnp.exp(x - mx) for x in lses]
    inv = 1.0 / functools.reduce(lambda a, b: a + b, es)
    alphas = [e * inv for e in es]
    dh = DSA_HEAD_DIM
    for hh in range(heads):
        cols = slice(hh * dh, (hh + 1) * dh)
        acc = None
        for g in range(groups):
            to_sequence_order(so_ref, g,
                              lambda rows, g=g: o_refs[g][0, rows, cols].astype(F32))
            t = alphas[g][:, hh:hh + 1] * so_ref[g]
            acc = t if acc is None else acc + t
        out_ref[0, :, cols] = acc.astype(out_ref.dtype)


def merge_groups(outs, lses, dils, *, heads):
    bsz, seq, hw = outs[0].shape
    groups = len(outs)
    assert DSA_HEAD_DIM == V7X_LANES
    out = pl.pallas_call(
        functools.partial(_merge_kernel, dils=tuple(dils), heads=heads),
        out_shape=jax.ShapeDtypeStruct((bsz, seq, hw), BF16),
        grid=(bsz,),
        in_specs=[pl.BlockSpec((1, seq, hw), lambda b: (b, 0, 0))] * groups
        + [pl.BlockSpec((1, seq, V7X_LANES), lambda b: (b, 0, 0))] * groups,
        out_specs=pl.BlockSpec((1, seq, hw), lambda b: (b, 0, 0)),
        scratch_shapes=[pltpu.VMEM((groups, seq, V7X_LANES), F32),
                        pltpu.VMEM((groups, seq, V7X_LANES), F32)],
        compiler_params=_params("parallel"),
        name="merge_groups",
    )(*outs, *lses)
    return out.reshape(bsz * seq, hw)


def _mla_attn_kernel(q_ref, kv_ref, kpe_ref, o_ref, *, bk):
    qi = pl.program_id(2)
    bq = q_ref.shape[1]
    steps = bq // bk
    rq = lax.broadcasted_iota(jnp.int32, (bq, bk), 0)
    ck = lax.broadcasted_iota(jnp.int32, (bq, bk), 1)

    def scores(q, kb):
        rows = slice(kb * bk, (kb + 1) * bk)
        k = jnp.concatenate([kv_ref[0, rows, :MLA_NOPE],
                             kpe_ref[0, rows, :].astype(BF16)], axis=1)
        return lax.dot_general(q, k, (((1,), (1,)), ((), ())), preferred_element_type=F32)

    def update(m_i, l_i, acc, s, kb):
        v = kv_ref[0, kb * bk:(kb + 1) * bk, MLA_NOPE:]
        m_new = jnp.maximum(m_i, jnp.max(s, axis=-1, keepdims=True))
        alpha = jnp.exp2(m_i - m_new)
        p = jnp.exp2(s - m_new)
        l_new = alpha * l_i + jnp.sum(p, axis=-1, keepdims=True)
        acc = alpha * acc + jnp.dot(p.astype(BF16), v, preferred_element_type=F32)
        return m_new, l_new, acc

    def attend(n_full):
        q = q_ref[0]
        n_blocks = n_full + steps
        m_i = jnp.full((bq, 1), NEG_INF, F32)
        l_i = jnp.zeros((bq, 1), F32)
        acc = jnp.zeros((bq, MLA_V), F32)
        s = scores(q, 0)
        for kb in range(n_blocks):
            s_next = scores(q, kb + 1) if kb + 1 < n_blocks else None
            if kb >= n_full:
                s = jnp.where(ck + (kb - n_full) * bk <= rq, s, NEG_INF)
            m_i, l_i, acc = update(m_i, l_i, acc, s, kb)
            s = s_next
        o_ref[0] = (acc / l_i).astype(o_ref.dtype)

    for blk in range(kv_ref.shape[1] // bq):
        pl.when(qi == blk)(functools.partial(attend, blk * steps))


def mla_attention(q, kv, c, *, bsz, seq, kpe_block):
    hq = MLA_NOPE + V7X_LANES
    heads = q.shape[1] // hq
    bq = _tile(seq, MLA_BQ)
    bk = _tile(bq, MLA_BK)
    q3 = q.reshape(bsz, seq, heads * hq)
    kv3 = kv.reshape(bsz, seq, kv.shape[1])
    c3 = c.reshape(bsz, seq, c.shape[1])
    o = pl.pallas_call(
        functools.partial(_mla_attn_kernel, bk=bk),
        out_shape=jax.ShapeDtypeStruct((bsz, seq, heads * MLA_V), BF16),
        grid=(bsz, heads, seq // bq),
        in_specs=[pl.BlockSpec((1, bq, hq), lambda b, h, i: (b, i, h)),
                  pl.BlockSpec((1, seq, MLA_NOPE + MLA_V), lambda b, h, i: (b, 0, h)),
                  pl.BlockSpec((1, seq, V7X_LANES), lambda b, h, i: (b, 0, kpe_block))],
        out_specs=pl.BlockSpec((1, bq, MLA_V), lambda b, h, i: (b, i, h)),
        compiler_params=_params("parallel", "parallel", "parallel"),
        name="mla_attn",
    )(q3, kv3, c3)
    return o.reshape(bsz * seq, heads * MLA_V)


def _sgu_out_kernel(u_ref, v_ref, st_ref, lng_ref, lnb_ref, ws_ref, bs_ref, wo_ref, bo_ref,
                    h_ref, o_ref, acc_ref, gate_ref, *, width):
    g = pl.program_id(1)
    ng = pl.num_programs(1)
    bm = u_ref.shape[0]
    ch = ws_ref.shape[1]
    st = st_ref[...]
    mu = st[:, 0:1] * (1.0 / width)
    var = st[:, 1:2] * (1.0 / width) - mu * mu
    rstd = lax.rsqrt(var + NORM_EPS)
    ti = lax.broadcasted_iota(jnp.int32, (ch, ch), 0)
    si = lax.broadcasted_iota(jnp.int32, (ch, ch), 1)
    ws = jnp.where(si <= ti, ws_ref[0], 0.0).astype(BF16)
    bs = bs_ref[0]
    lng = lng_ref[...]
    lnb = lnb_ref[...]
    for c in range(bm // ch):
        rows = slice(c * ch, (c + 1) * ch)
        vn = (v_ref[rows, :].astype(F32) - mu[rows]) * rstd[rows] * lng + lnb
        mixed = jnp.dot(ws, vn.astype(BF16), preferred_element_type=F32) + bs
        gate_ref[rows, :] = (u_ref[rows, :].astype(F32) * mixed).astype(BF16)
    part = jnp.dot(gate_ref[...], wo_ref[...], preferred_element_type=F32)

    @pl.when(g == 0)
    def _():
        acc_ref[...] = part

    @pl.when(g > 0)
    def _():
        acc_ref[...] += part

    @pl.when(g == ng - 1)
    def _():
        o_ref[...] = acc_ref[...] + bo_ref[...] + h_ref[...]


def sgu_out(u, v, st, ln_g, ln_b, w_s, b_s, w_out, b_out, h):
    m, e = u.shape
    d = w_out.shape[1]
    groups, ch, _ = w_s.shape
    gw = e // groups
    bm = _tile(m, SGU_BM)
    assert bm % ch == 0
    return pl.pallas_call(
        functools.partial(_sgu_out_kernel, width=e),
        out_shape=jax.ShapeDtypeStruct((m, d), F32),
        grid=(m // bm, groups),
        in_specs=[
            pl.BlockSpec((bm, gw), lambda i, g: (i, g)),
            pl.BlockSpec((bm, gw), lambda i, g: (i, g)),
            pl.BlockSpec((bm, V7X_LANES), lambda i, g: (i, 0)),
            pl.BlockSpec((1, gw), lambda i, g: (0, g)),
            pl.BlockSpec((1, gw), lambda i, g: (0, g)),
            pl.BlockSpec((1, ch, ch), lambda i, g: (g, 0, 0)),
            pl.BlockSpec((1, ch, 1), lambda i, g: (g, 0, 0)),
            pl.BlockSpec((gw, d), lambda i, g: (g, 0)),
            pl.BlockSpec((1, d), lambda i, g: (0, 0)),
            pl.BlockSpec((bm, d), lambda i, g: (i, 0)),
        ],
        out_specs=pl.BlockSpec((bm, d), lambda i, g: (i, 0)),
        scratch_shapes=[pltpu.VMEM((bm, d), F32), pltpu.VMEM((bm, gw), BF16)],
        compiler_params=_params("parallel", "arbitrary"),
        name="sgu_out",
    )(u, v, st, ln_g.reshape(1, e).astype(F32), ln_b.reshape(1, e).astype(F32),
      w_s.astype(F32), b_s.reshape(groups, ch, 1).astype(F32), w_out.astype(BF16),
      b_out.reshape(1, d).astype(F32), h)


def _rope_tables(seq, dim):
    half = dim // 2
    pos = jnp.arange(seq, dtype=F32)
    inv = ROPE_THETA ** (-(jnp.arange(0, dim, 2, dtype=F32) / dim))
    ang = pos[:, None] * inv[None, :]
    zeros = jnp.zeros((seq, V7X_LANES // 2 - half), F32)
    cos = jnp.concatenate([jnp.cos(ang), zeros, jnp.cos(ang), zeros], axis=1)
    sin = jnp.concatenate([-jnp.sin(ang), zeros, jnp.sin(ang), zeros], axis=1)
    return cos, sin


def _spread_rope_cols(w):
    k, dim = w.shape
    half = dim // 2
    zeros = jnp.zeros((k, V7X_LANES // 2 - half), w.dtype)
    return jnp.concatenate([w[:, :half], zeros, w[:, half:], zeros], axis=1)


def conformer_mixer(xn, h, w_in, b_in, dw_w, dw_b, ln_g, ln_b, w_out, b_out, *, bsz, seq):
    c = w_out.shape[0]
    wi = w_in.astype(BF16)
    z = matmul(xn, wi[:, :c], w2=wi[:, c:], bias=b_in[:c], bias2=b_in[c:])
    z = causal_dwconv(z.reshape(bsz, seq, c), dw_w, dw_b).reshape(bsz * seq, c)
    return matmul(z, w_out.astype(BF16), prologue="ln_silu", pg=ln_g, pb=ln_b, bias=b_out,
                  residual=h, out_dtype=F32)


def dilated_attention_mixer(xn, h, w_qkv, w_o, *, bsz, seq):
    d = xn.shape[1]
    dh = DSA_HEAD_DIM
    heads = w_o.shape[0] // dh
    groups = len(DSA_CONFIGS)
    hw = heads * dh
    w5 = w_qkv.reshape(d, groups, 3, hw)
    scale = jnp.array([dh ** -0.5, 1.0, 1.0], F32).reshape(1, 1, 3, 1)
    w5 = (w5 * scale).astype(BF16)
    tables = _rope_tables(seq, dh)
    outs, lses = [], []
    for g, (window, dil) in enumerate(DSA_CONFIGS):
        qkv = dsa_project(xn, w5[:, g].reshape(d, 3 * hw), tables, dil, bsz=bsz, seq=seq,
                          rope_cols=2 * hw)
        o, lse = band_attention(qkv, window, dil, bsz=bsz, seq=seq, heads=heads)
        outs.append(o)
        lses.append(lse)
    o = merge_groups(outs, lses, [dil for _, dil in DSA_CONFIGS], heads=heads)
    return matmul(o, w_o.astype(BF16), residual=h, out_dtype=F32)


def mla_mixer(xn, h, w_in, q_norm, w_qb, kv_norm, w_kvb, w_o, *, bsz, seq):
    q_rank = q_norm.shape[0]
    kv_rank = kv_norm.shape[0]
    assert q_rank == kv_rank
    heads = MLA_HEADS
    w_in_p = jnp.concatenate(
        [w_in[:, :q_rank + kv_rank], _spread_rope_cols(w_in[:, q_rank + kv_rank:])],
        axis=1).astype(BF16)
    n_c = w_in_p.shape[1]
    kpe_group = (q_rank + kv_rank) // V7X_LANES
    cmask = tuple(k == kpe_group for k in range(n_c // V7X_LANES))
    tables = _rope_tables(seq, MLA_ROPE)
    c = matmul(xn, w_in_p, rope=tables, rope_mask=cmask, seq=seq, out_dtype=F32, bn=n_c)
    q_scale = (MLA_NOPE + MLA_ROPE) ** -0.5 * math.log2(math.e)
    wq3 = w_qb.reshape(q_rank, heads, MLA_NOPE + MLA_ROPE) * q_scale
    wq_pe = jax.vmap(_spread_rope_cols, in_axes=1, out_axes=1)(wq3[:, :, MLA_NOPE:])
    wq_p = jnp.concatenate([wq3[:, :, :MLA_NOPE], wq_pe], axis=2)
    wq_p = wq_p.reshape(q_rank, heads * (MLA_NOPE + V7X_LANES)).astype(BF16)
    bn = _tile(wq_p.shape[1], MLA_PROJ_BN)
    qmask = tuple(k % 2 == 1 for k in range(bn // V7X_LANES))
    q = matmul(c, wq_p, x_cols=(0, q_rank), prologue="rms", pg=q_norm, rope=tables,
               rope_mask=qmask, seq=seq, bm=MM_BM, bn=bn)
    kv = matmul(c, w_kvb.astype(BF16), x_cols=(1, kv_rank), prologue="rms", pg=kv_norm,
                bm=MM_BM, bn=MLA_PROJ_BN)
    o = mla_attention(q, kv, c, bsz=bsz, seq=seq, kpe_block=kpe_group)
    return matmul(o, w_o.astype(BF16), residual=h, out_dtype=F32)


def sgu_mixer(xn, h, w_in, b_in, ln_g, ln_b, w_s, b_s, w_out, b_out):
    e = w_out.shape[0]
    wi = w_in.astype(BF16)
    u = matmul(xn, wi[:, :e], bias=b_in[:e], gelu=True)
    v, st = matmul(xn, wi[:, e:], bias=b_in[e:], gelu=True, stats=True)
    return sgu_out(u, v, st, ln_g, ln_b, w_s, b_s, w_out, b_out, h)


def kernel(x, l0_norm_mix, l0_cc_w_in, l0_cc_b_in, l0_cc_dw_w, l0_cc_dw_b, l0_cc_ln_g, l0_cc_ln_b, l0_cc_w_out, l0_cc_b_out, l0_norm_ffn, l0_ffn_w_up, l0_ffn_dw_w, l0_ffn_dw_b, l0_ffn_w_down, l1_norm_mix, l1_dsa_w_qkv, l1_dsa_w_o, l1_norm_ffn, l1_ffn_w_up, l1_ffn_dw_w, l1_ffn_dw_b, l1_ffn_w_down, l2_norm_mix, l2_mla_w_in, l2_mla_q_norm, l2_mla_w_qb, l2_mla_kv_norm, l2_mla_w_kvb, l2_mla_w_o, l2_norm_ffn, l2_ffn_w_up, l2_ffn_dw_w, l2_ffn_dw_b, l2_ffn_w_down, l3_norm_mix, l3_sg_w_in, l3_sg_b_in, l3_sg_ln_g, l3_sg_ln_b, l3_sg_w_s, l3_sg_b_s, l3_sg_w_out, l3_sg_b_out, l3_norm_ffn, l3_ffn_w_up, l3_ffn_dw_w, l3_ffn_dw_b, l3_ffn_w_down, final_norm):
    bsz, seq, d = x.shape
    dims = dict(bsz=bsz, seq=seq)
    h = x.reshape(bsz * seq, d)

    def ffn(h, norm, w_up, dw_w, dw_b, w_down, out_g=None):
        return conv_ffn(h, norm, w_up, dw_w, dw_b, w_down, seq=seq, out_g=out_g)

    h = conformer_mixer(rmsnorm(h, l0_norm_mix, BF16), h, l0_cc_w_in, l0_cc_b_in, l0_cc_dw_w,
                        l0_cc_dw_b, l0_cc_ln_g, l0_cc_ln_b, l0_cc_w_out, l0_cc_b_out, **dims)
    h = ffn(h, l0_norm_ffn, l0_ffn_w_up, l0_ffn_dw_w, l0_ffn_dw_b, l0_ffn_w_down)
    h = dilated_attention_mixer(rmsnorm(h, l1_norm_mix, BF16), h, l1_dsa_w_qkv, l1_dsa_w_o,
                                **dims)
    h = ffn(h, l1_norm_ffn, l1_ffn_w_up, l1_ffn_dw_w, l1_ffn_dw_b, l1_ffn_w_down)
    h = mla_mixer(rmsnorm(h, l2_norm_mix, BF16), h, l2_mla_w_in, l2_mla_q_norm, l2_mla_w_qb,
                  l2_mla_kv_norm, l2_mla_w_kvb, l2_mla_w_o, **dims)
    h = ffn(h, l2_norm_ffn, l2_ffn_w_up, l2_ffn_dw_w, l2_ffn_dw_b, l2_ffn_w_down)
    h = sgu_mixer(rmsnorm(h, l3_norm_mix, BF16), h, l3_sg_w_in, l3_sg_b_in, l3_sg_ln_g,
                  l3_sg_ln_b, l3_sg_w_s, l3_sg_b_s, l3_sg_w_out, l3_sg_b_out)
    out = ffn(h, l3_norm_ffn, l3_ffn_w_up, l3_ffn_dw_w, l3_ffn_dw_b, l3_ffn_w_down,
              out_g=final_norm)
    return out.reshape(bsz, seq, d)
```

```python
import functools
import math

import jax
import jax.numpy as jnp
from jax import lax
from jax.experimental import pallas as pl
from jax.experimental.pallas import tpu as pltpu

F32 = jnp.float32
BF16 = jnp.bfloat16

NORM_EPS = 1e-6
ROPE_THETA = 10000.0
NEG_INF = -1e30

DSA_HEAD_DIM = 128
DSA_CONFIGS = ((128, 1), (512, 4), (2048, 16))
DSA_BLOCK = 128
MLA_HEADS = 16
MLA_NOPE = 128
MLA_ROPE = 64
MLA_V = 128
SG_CHUNK = 128
SG_GROUPS = 8

V7X_LANES = 128
V7X_SUBLANES = 8
V7X_VMEM_LIMIT_BYTES = 56 * 1024 * 1024

MM_BM, MM_BN = 1024, 1024
MM_PROLOGUE_BM = 512
FFN_BM, FFN_FC = 512, 512
FFN_SUB, FFN_RSUB = 256, 256
MLA_BQ, MLA_BK = 512, 512
SGU_BM = 512
CONV_CB, CONV_ROWS = 256, 64
DSA_PROJ_BN = 512
MLA_PROJ_BN = 2048


def _params(*semantics):
    return pltpu.CompilerParams(dimension_semantics=semantics,
                                vmem_limit_bytes=V7X_VMEM_LIMIT_BYTES)


def _sigmoid(x):
    return 1.0 / (1.0 + jnp.exp(-x))


def _tile(n, want):
    t = min(n, want)
    while n % t:
        t -= 1
    return t


def _rmsnorm_kernel(x_ref, g_ref, o_ref):
    x = x_ref[...].astype(F32)
    y = x * lax.rsqrt(jnp.mean(x * x, axis=-1, keepdims=True) + NORM_EPS)
    o_ref[...] = (y * g_ref[...]).astype(o_ref.dtype)


def rmsnorm(x, g, out_dtype):
    m, d = x.shape
    bm = _tile(m, 512)
    return pl.pallas_call(
        _rmsnorm_kernel,
        out_shape=jax.ShapeDtypeStruct((m, d), out_dtype),
        grid=(m // bm,),
        in_specs=[pl.BlockSpec((bm, d), lambda i: (i, 0)),
                  pl.BlockSpec((1, d), lambda i: (0, 0))],
        out_specs=pl.BlockSpec((bm, d), lambda i: (i, 0)),
        compiler_params=_params("parallel"),
        name="rmsnorm",
    )(x, g.reshape(1, d).astype(F32))


def _mm_kernel(*refs, prologue, has_bias, glu, gelu, rope_mask, residual, stats):
    it = iter(refs)
    x_ref = next(it)
    pg_ref = next(it) if prologue else None
    pb_ref = next(it) if prologue == "ln_silu" else None
    w_ref = next(it)
    w2_ref = next(it) if glu else None
    b_ref = next(it) if has_bias else None
    b2_ref = next(it) if (glu and has_bias) else None
    cos_ref = next(it) if rope_mask else None
    sin_ref = next(it) if rope_mask else None
    res_ref = next(it) if residual else None
    o_ref = next(it)
    st_ref = next(it) if stats else None
    xs_ref = next(it) if prologue else None

    j = pl.program_id(1)

    if prologue:
        @pl.when(j == 0)
        def _():
            xf = x_ref[...].astype(F32)
            if prologue == "rms":
                y = xf * lax.rsqrt(jnp.mean(xf * xf, axis=-1, keepdims=True) + NORM_EPS)
                y = y * pg_ref[...]
            else:
                mu = jnp.mean(xf, axis=-1, keepdims=True)
                xc = xf - mu
                y = xc * lax.rsqrt(jnp.mean(xc * xc, axis=-1, keepdims=True) + NORM_EPS)
                y = y * pg_ref[...] + pb_ref[...]
                y = y * _sigmoid(y)
            xs_ref[...] = y.astype(BF16)
        xv = xs_ref[...]
    else:
        xv = x_ref[...]

    acc = jnp.dot(xv, w_ref[...], preferred_element_type=F32)
    if has_bias:
        acc = acc + b_ref[...]
    if glu:
        gate = jnp.dot(xv, w2_ref[...], preferred_element_type=F32)
        if has_bias:
            gate = gate + b2_ref[...]
        acc = acc * _sigmoid(gate)
    if gelu:
        acc = 0.5 * acc * (1.0 + lax.erf(acc * (2.0 ** -0.5)))
    if stats:
        s1 = jnp.sum(acc, axis=-1, keepdims=True)
        s2 = jnp.sum(acc * acc, axis=-1, keepdims=True)
        lane = lax.broadcasted_iota(jnp.int32, st_ref.shape, 1)
        upd = jnp.where(lane == 0, s1, jnp.where(lane == 1, s2, 0.0))

        @pl.when(j == 0)
        def _():
            st_ref[...] = upd

        @pl.when(j > 0)
        def _():
            st_ref[...] += upd
    if residual:
        acc = acc + res_ref[...]
    if rope_mask:
        cos = cos_ref[...]
        sin = sin_ref[...]
        for k, on in enumerate(rope_mask):
            sl = acc[:, k * V7X_LANES:(k + 1) * V7X_LANES]
            if on:
                sl = sl * cos + pltpu.roll(sl, V7X_LANES // 2, 1) * sin
            o_ref[:, k * V7X_LANES:(k + 1) * V7X_LANES] = sl.astype(o_ref.dtype)
    else:
        o_ref[...] = acc.astype(o_ref.dtype)


def matmul(x, w, *, x_cols=None, prologue=None, pg=None, pb=None, w2=None, bias=None,
           bias2=None, gelu=False, rope=None, rope_mask=None, seq=None, residual=None,
           stats=False, out_dtype=BF16, bm=None, bn=None):
    m = x.shape[0]
    k, n = w.shape
    kblk = 0 if x_cols is None else x_cols[0]
    bm = _tile(m, bm or (MM_PROLOGUE_BM if prologue else MM_BM))
    bn = _tile(n, bn or MM_BN)
    glu = w2 is not None
    has_bias = bias is not None
    if rope_mask:
        assert bn % V7X_LANES == 0 and len(rope_mask) == bn // V7X_LANES
        assert seq % bm == 0
    args, specs = [x], [pl.BlockSpec((bm, k), lambda i, j: (i, kblk))]
    if prologue:
        args.append(pg.reshape(1, k).astype(F32))
        specs.append(pl.BlockSpec((1, k), lambda i, j: (0, 0)))
        if prologue == "ln_silu":
            args.append(pb.reshape(1, k).astype(F32))
            specs.append(pl.BlockSpec((1, k), lambda i, j: (0, 0)))
    args.append(w)
    specs.append(pl.BlockSpec((k, bn), lambda i, j: (0, j)))
    if glu:
        args.append(w2)
        specs.append(pl.BlockSpec((k, bn), lambda i, j: (0, j)))
    if has_bias:
        args.append(bias.reshape(1, n).astype(F32))
        specs.append(pl.BlockSpec((1, bn), lambda i, j: (0, j)))
        if glu:
            args.append(bias2.reshape(1, n).astype(F32))
            specs.append(pl.BlockSpec((1, bn), lambda i, j: (0, j)))
    if rope_mask:
        tiles_per_seq = seq // bm
        for t in rope:
            args.append(t)
            specs.append(pl.BlockSpec((bm, V7X_LANES), lambda i, j: (i % tiles_per_seq, 0)))
    if residual is not None:
        args.append(residual)
        specs.append(pl.BlockSpec((bm, bn), lambda i, j: (i, j)))
    out_shape = [jax.ShapeDtypeStruct((m, n), out_dtype)]
    out_specs = [pl.BlockSpec((bm, bn), lambda i, j: (i, j))]
    if stats:
        out_shape.append(jax.ShapeDtypeStruct((m, V7X_LANES), F32))
        out_specs.append(pl.BlockSpec((bm, V7X_LANES), lambda i, j: (i, 0)))
    scratch = [pltpu.VMEM((bm, k), BF16)] if prologue else []
    kern = functools.partial(
        _mm_kernel, prologue=prologue, has_bias=has_bias, glu=glu, gelu=gelu,
        rope_mask=tuple(rope_mask) if rope_mask else None,
        residual=residual is not None, stats=stats)
    out = pl.pallas_call(
        kern,
        out_shape=out_shape,
        grid=(m // bm, n // bn),
        in_specs=specs,
        out_specs=out_specs,
        scratch_shapes=scratch,
        compiler_params=_params("parallel", "arbitrary"),
        name="matmul",
    )(*args)
    return out if stats else out[0]


def _dwconv_kernel(x_ref, w_ref, b_ref, o_ref, sh_ref, *, width, row_chunk):
    s, cb = x_ref.shape[1], x_ref.shape[2]
    pad = sh_ref.shape[1] - s
    x = x_ref[0].astype(F32)
    xz = jnp.concatenate([x, jnp.zeros((V7X_SUBLANES, cb), F32)], axis=0)
    zero_top = jnp.zeros((pad, cb), F32)
    for r in range(V7X_SUBLANES):
        sh_ref[r, :pad, :] = zero_top
        shifted = xz if r == 0 else pltpu.roll(xz, r, 0)
        sh_ref[r, pad:, :] = shifted[:s]

    w = w_ref[...]
    bias = b_ref[...]

    def chunk(c, carry):
        base = pl.multiple_of(c * row_chunk, row_chunk)
        acc = jnp.zeros((row_chunk, cb), F32) + bias
        for kk in range(width):
            shift = width - 1 - kk
            q, r = divmod(shift, V7X_SUBLANES)
            start = pl.multiple_of(base + (pad - q * V7X_SUBLANES), V7X_SUBLANES)
            tap = sh_ref[r, pl.ds(start, row_chunk), :]
            acc = acc + tap * w[kk:kk + 1, :]
        o_ref[0, pl.ds(base, row_chunk), :] = acc.astype(o_ref.dtype)
        return carry

    lax.fori_loop(0, s // row_chunk, chunk, 0)


def causal_dwconv(x, w, b):
    bsz, s, c = x.shape
    width = w.shape[0]
    cb = _tile(c, CONV_CB)
    row_chunk = _tile(s, CONV_ROWS)
    pad = -(-(width - 1) // V7X_SUBLANES) * V7X_SUBLANES
    return pl.pallas_call(
        functools.partial(_dwconv_kernel, width=width, row_chunk=row_chunk),
        out_shape=jax.ShapeDtypeStruct((bsz, s, c), BF16),
        grid=(bsz, c // cb),
        in_specs=[pl.BlockSpec((1, s, cb), lambda i, j: (i, 0, j)),
                  pl.BlockSpec((width, cb), lambda i, j: (0, j)),
                  pl.BlockSpec((1, cb), lambda i, j: (0, j))],
        out_specs=pl.BlockSpec((1, s, cb), lambda i, j: (i, 0, j)),
        scratch_shapes=[pltpu.VMEM((V7X_SUBLANES, pad + s, cb), F32)],
        compiler_params=_params("parallel", "parallel"),
        name="dwconv",
    )(x, w.astype(F32), b.reshape(1, c).astype(F32))


def _ffn_kernel(h_ref, ng_ref, og_ref, wg_ref, wa_ref, cwg_ref, cwa_ref, cbg_ref, cba_ref,
                wd_ref, o_ref, *rest, tiles_per_seq, nf, sub, rsub, tail):
    xn_ref = rest[0] if tail == "residual_and_norm" else None
    xs_ref, acc_ref, carry_g_ref, carry_a_ref = rest[-4:]
    i = pl.program_id(0)
    j = pl.program_id(1)
    bm, d = h_ref.shape
    fc = wg_ref.shape[1]
    seq_start = (i % tiles_per_seq) == 0
    sl = V7X_SUBLANES
    nv = rsub // sl
    slabs = [slice(c * V7X_LANES, (c + 1) * V7X_LANES) for c in range(d // V7X_LANES)]

    na = nv // sl

    def staged(r, a, k):
        return pl.ds(r * rsub + sl * sl * a + k, sl, stride=sl)

    @pl.when(seq_start)
    def _():
        carry_g_ref[j] = jnp.zeros((2 * sl, fc), F32)
        carry_a_ref[j] = jnp.zeros((2 * sl, fc), F32)

    def normalise(r):
        gain = ng_ref[...]
        for s in range(sl):
            for a in range(na):
                src = r * rsub + nv * s + sl * a
                dst = r * rsub + sl * sl * a + sl * s
                hv = h_ref[src:src + sl, :]
                y = hv * lax.rsqrt(jnp.mean(hv * hv, axis=-1, keepdims=True) + NORM_EPS) * gain
                for c, cols in enumerate(slabs):
                    acc_ref[c, dst:dst + sl, :] = y[:, cols]
        for a in range(na):
            for b in range(0, sl, 2):
                dst = r * rsub + sl * (sl * a + b)
                for c, cols in enumerate(slabs):
                    pair = jnp.concatenate([acc_ref[c, staged(r, a, b), :],
                                            acc_ref[c, staged(r, a, b + 1), :]], axis=0)
                    xs_ref[dst:dst + 2 * sl, cols] = pair.astype(BF16)
        for c in range(len(slabs)):
            acc_ref[c, r * rsub:(r + 1) * rsub, :] = jnp.zeros((rsub, V7X_LANES), F32)

    def finish(r):
        for s in range(sl):
            for a in range(0, na, 2):
                dst = r * rsub + nv * s + sl * a
                rows = slice(dst, dst + 2 * sl)
                vals = [jnp.concatenate([acc_ref[c, staged(r, a, s), :],
                                         acc_ref[c, staged(r, a + 1, s), :]], axis=0)
                        + h_ref[rows, cols] for c, cols in enumerate(slabs)]
                if tail != "residual":
                    ss = functools.reduce(
                        lambda p, q: p + q,
                        [jnp.sum(v * v, axis=-1, keepdims=True) for v in vals])
                    inv = lax.rsqrt(ss * (1.0 / d) + NORM_EPS)
                    normed = [v * inv * og_ref[:, cols] for v, cols in zip(vals, slabs)]
                if tail == "norm_only":
                    vals = normed
                for v, cols in zip(vals, slabs):
                    o_ref[rows, cols] = v
                if tail == "residual_and_norm":
                    for v, cols in zip(normed, slabs):
                        xn_ref[rows, cols] = v.astype(BF16)

    first_row = lax.broadcasted_iota(jnp.int32, (sl, sub), 0) == 0

    def shift1(z, prev_row):
        top = jnp.where(first_row, prev_row, pltpu.roll(z[rsub - sl:, :], 1, 0))
        return jnp.concatenate([top, z[:rsub - sl, :]], axis=0)

    def conv(z, prev, cw, cb):
        z1 = shift1(z, prev[2 * sl - 1:2 * sl, :])
        z2 = shift1(z1, prev[sl - 1:sl, :])
        return z2 * cw[0:1, :] + z1 * cw[1:2, :] + z * cw[2:3, :] + cb

    n_r = bm // rsub
    tiles = [(r, slice(c * sub, (c + 1) * sub)) for c in range(fc // sub) for r in range(n_r)]

    def up(r, cols):
        xr = xs_ref[r * rsub:(r + 1) * rsub, :]
        return (jnp.dot(xr, wg_ref[:, cols], preferred_element_type=F32),
                jnp.dot(xr, wa_ref[:, cols], preferred_element_type=F32))

    def step(first, last):
        if first:
            for r in range(n_r):
                normalise(r)
        z_next = up(*tiles[0])
        for k, (r, cols) in enumerate(tiles):
            rows = slice(r * rsub, (r + 1) * rsub)
            zg, za = z_next
            if k + 1 < len(tiles):
                z_next = up(*tiles[k + 1])
            if r == 0:
                prev_g = carry_g_ref[j, :, cols]
                prev_a = carry_a_ref[j, :, cols]
            g = conv(zg, prev_g, cwg_ref[:, cols], cbg_ref[:, cols])
            a = conv(za, prev_a, cwa_ref[:, cols], cba_ref[:, cols])
            prev_g = zg[rsub - 2 * sl:, :]
            prev_a = za[rsub - 2 * sl:, :]
            if r == n_r - 1:
                carry_g_ref[j, :, cols] = prev_g
                carry_a_ref[j, :, cols] = prev_a
            hg = 0.5 * g
            act = ((hg + hg * jnp.tanh(hg)) * a).astype(BF16)
            part = jnp.dot(act, wd_ref[cols, :], preferred_element_type=F32)
            for c, dcols in enumerate(slabs):
                acc_ref[c, rows, :] += part[:, dcols]
            if last and k >= len(tiles) - n_r:
                finish(r)

    if nf == 1:
        step(True, True)
    else:
        pl.when(j == 0)(functools.partial(step, True, False))
        pl.when((j > 0) & (j < nf - 1))(functools.partial(step, False, False))
        pl.when(j == nf - 1)(functools.partial(step, False, True))


def conv_ffn(h, norm_g, w_up, dw_w, dw_b, w_down, *, seq, tail="residual", tail_g=None):
    m, d = h.shape
    f = w_down.shape[0]
    bm = _tile(seq, FFN_BM)
    fc = _tile(f, FFN_FC)
    nf = f // fc
    wu = w_up.astype(BF16)
    wd = w_down.astype(BF16)
    cw = dw_w.astype(F32)
    cb = dw_b.reshape(1, 2 * f).astype(F32)
    rsub = _tile(bm, FFN_RSUB)
    assert rsub % (2 * V7X_SUBLANES * V7X_SUBLANES) == 0 and d % V7X_LANES == 0
    kern = functools.partial(_ffn_kernel, tiles_per_seq=seq // bm, nf=nf,
                             sub=_tile(fc, FFN_SUB), rsub=rsub, tail=tail)
    gains = [g.reshape(1, d).astype(F32) for g in (norm_g, norm_g if tail_g is None else tail_g)]
    row_tile = pl.BlockSpec((bm, d), lambda i, j: (i, 0))
    out_shape = [jax.ShapeDtypeStruct((m, d), F32)]
    if tail == "residual_and_norm":
        out_shape.append(jax.ShapeDtypeStruct((m, d), BF16))
    out = pl.pallas_call(
        kern,
        out_shape=out_shape,
        grid=(m // bm, nf),
        in_specs=[
            row_tile,
            pl.BlockSpec((1, d), lambda i, j: (0, 0)),
            pl.BlockSpec((1, d), lambda i, j: (0, 0)),
            pl.BlockSpec((d, fc), lambda i, j: (0, j)),
            pl.BlockSpec((d, fc), lambda i, j: (0, j + nf)),
            pl.BlockSpec((3, fc), lambda i, j: (0, j)),
            pl.BlockSpec((3, fc), lambda i, j: (0, j + nf)),
            pl.BlockSpec((1, fc), lambda i, j: (0, j)),
            pl.BlockSpec((1, fc), lambda i, j: (0, j + nf)),
            pl.BlockSpec((fc, d), lambda i, j: (j, 0)),
        ],
        out_specs=[row_tile] * len(out_shape),
        scratch_shapes=[
            pltpu.VMEM((bm, d), BF16),
            pltpu.VMEM((d // V7X_LANES, bm, V7X_LANES), F32),
            pltpu.VMEM((nf, 2 * V7X_SUBLANES, fc), F32),
            pltpu.VMEM((nf, 2 * V7X_SUBLANES, fc), F32),
        ],
        compiler_params=_params("arbitrary", "arbitrary"),
        name="conv_ffn",
    )(h, *gains, wu, wu, cw, cw, cb, cb, wd)
    return out if tail == "residual_and_norm" else out[0]


def _band_attn_kernel(q_ref, kp_ref, kc_ref, vp_ref, vc_ref, o_ref, lse_ref, *, heads, span):
    n = pl.program_id(2)
    blk = q_ref.shape[1]
    qi = lax.broadcasted_iota(jnp.int32, (blk, 2 * blk), 0)
    kj = lax.broadcasted_iota(jnp.int32, (blk, 2 * blk), 1)
    dist = blk + qi - kj
    first_key = jnp.where(n > 0, 0, blk)
    mask = (dist >= 0) & (dist <= span) & (kj >= first_key)
    lane = lax.broadcasted_iota(jnp.int32, (blk, V7X_LANES), 1)
    lse_tile = jnp.zeros((blk, V7X_LANES), F32)
    dh = DSA_HEAD_DIM
    for hh in range(heads):
        cols = slice(hh * dh, (hh + 1) * dh)
        q = q_ref[0, :, cols]
        k = jnp.concatenate([kp_ref[0, :, cols], kc_ref[0, :, cols]], axis=0)
        v = jnp.concatenate([vp_ref[0, :, cols], vc_ref[0, :, cols]], axis=0)
        s = lax.dot_general(q, k, (((1,), (1,)), ((), ())), preferred_element_type=F32)
        s = jnp.where(mask, s, NEG_INF)
        mx = jnp.max(s, axis=-1, keepdims=True)
        p = jnp.exp(s - mx)
        l = jnp.sum(p, axis=-1, keepdims=True)
        o = jnp.dot(p.astype(BF16), v, preferred_element_type=F32) / l
        o_ref[0, :, cols] = o.astype(o_ref.dtype)
        lse_tile = jnp.where(lane == hh, mx + jnp.log(l), lse_tile)
    lse_ref[0] = lse_tile


def band_attention(qkv, window, dil, *, bsz, seq, heads):
    hw = heads * DSA_HEAD_DIM
    blk = DSA_BLOCK
    nb = seq // dil // blk
    span = window // dil
    qkv3 = qkv.reshape(bsz, seq, 3 * hw)
    cur = lambda r, n: r * nb + n
    prev = lambda r, n: r * nb + jnp.maximum(n - 1, 0)
    return pl.pallas_call(
        functools.partial(_band_attn_kernel, heads=heads, span=span),
        out_shape=[jax.ShapeDtypeStruct((bsz, seq, hw), BF16),
                   jax.ShapeDtypeStruct((bsz, seq, V7X_LANES), F32)],
        grid=(bsz, dil, nb),
        in_specs=[
            pl.BlockSpec((1, blk, hw), lambda b, r, n: (b, cur(r, n), 0)),
            pl.BlockSpec((1, blk, hw), lambda b, r, n: (b, prev(r, n), 1)),
            pl.BlockSpec((1, blk, hw), lambda b, r, n: (b, cur(r, n), 1)),
            pl.BlockSpec((1, blk, hw), lambda b, r, n: (b, prev(r, n), 2)),
            pl.BlockSpec((1, blk, hw), lambda b, r, n: (b, cur(r, n), 2)),
        ],
        out_specs=[pl.BlockSpec((1, blk, hw), lambda b, r, n: (b, cur(r, n), 0)),
                   pl.BlockSpec((1, blk, V7X_LANES), lambda b, r, n: (b, cur(r, n), 0))],
        compiler_params=_params("parallel", "parallel", "parallel"),
        name="band_attn",
    )(qkv3, qkv3, qkv3, qkv3, qkv3)


def _dsa_proj_kernel(x_ref, w_ref, cos_ref, sin_ref, o_ref, *scratch, dil, rope_tiles):
    j = pl.program_id(1)
    s = x_ref.shape[0]
    ln = s // dil
    acc = jnp.dot(x_ref[...], w_ref[...], preferred_element_type=F32)
    slabs = [slice(c * V7X_LANES, (c + 1) * V7X_LANES) for c in range(acc.shape[1] // V7X_LANES)]
    if dil > 1:
        stage_ref, = scratch
        for c, cols in enumerate(slabs):
            stage_ref[c] = acc[:, cols]

    def emit(rope):
        for r in range(dil):
            rows = slice(r * ln, (r + 1) * ln)
            for c, cols in enumerate(slabs):
                blk = stage_ref[c, pl.ds(r, ln, stride=dil), :] if dil > 1 else acc[:, cols]
                if rope:
                    blk = (blk * cos_ref[rows, :]
                           + pltpu.roll(blk, V7X_LANES // 2, 1) * sin_ref[rows, :])
                o_ref[rows, cols] = blk.astype(o_ref.dtype)

    @pl.when(j < rope_tiles)
    def _():
        emit(True)

    @pl.when(j >= rope_tiles)
    def _():
        emit(False)


def dsa_project(xn, w_g, tables, dil, *, bsz, seq, rope_cols):
    m, d = xn.shape
    n = w_g.shape[1]
    bn = _tile(n, DSA_PROJ_BN)
    assert rope_cols % bn == 0
    residue_major = lambda t: t.reshape(seq // dil, dil, V7X_LANES).transpose(1, 0, 2).reshape(
        seq, V7X_LANES)
    cos, sin = (residue_major(t) for t in tables)
    scratch = [pltpu.VMEM((bn // V7X_LANES, seq, V7X_LANES), F32)] if dil > 1 else []
    return pl.pallas_call(
        functools.partial(_dsa_proj_kernel, dil=dil, rope_tiles=rope_cols // bn),
        out_shape=jax.ShapeDtypeStruct((m, n), BF16),
        grid=(bsz, n // bn),
        in_specs=[pl.BlockSpec((seq, d), lambda b, j: (b, 0)),
                  pl.BlockSpec((d, bn), lambda b, j: (0, j)),
                  pl.BlockSpec((seq, V7X_LANES), lambda b, j: (0, 0)),
                  pl.BlockSpec((seq, V7X_LANES), lambda b, j: (0, 0))],
        out_specs=pl.BlockSpec((seq, bn), lambda b, j: (b, j)),
        scratch_shapes=scratch,
        compiler_params=_params("parallel", "arbitrary"),
        name="dsa_proj",
    )(xn, w_g, cos, sin)


def _merge_kernel(*refs, dils, heads):
    groups = len(dils)
    o_refs = refs[:groups]
    lse_refs = refs[groups:2 * groups]
    out_ref, so_ref, sl_ref = refs[2 * groups:]
    s = out_ref.shape[1]

    def to_sequence_order(dst_ref, g, rows_of):
        ln = s // dils[g]
        for r in range(dils[g]):
            dst_ref[g, pl.ds(r, ln, stride=dils[g]), :] = rows_of(slice(r * ln, (r + 1) * ln))

    for g in range(groups):
        to_sequence_order(sl_ref, g, lambda rows, g=g: lse_refs[g][0, rows, :])
    lses = [sl_ref[g] for g in range(groups)]
    mx = functools.reduce(jnp.maximum, lses)
    es = [jnp.exp(x - mx) for x in lses]
    inv = 1.0 / functools.reduce(lambda a, b: a + b, es)
    alphas = [e * inv for e in es]
    dh = DSA_HEAD_DIM
    for hh in range(heads):
        cols = slice(hh * dh, (hh + 1) * dh)
        acc = None
        for g in range(groups):
            to_sequence_order(so_ref, g,
                              lambda rows, g=g: o_refs[g][0, rows, cols].astype(F32))
            t = alphas[g][:, hh:hh + 1] * so_ref[g]
            acc = t if acc is None else acc + t
        out_ref[0, :, cols] = acc.astype(out_ref.dtype)


def merge_groups(outs, lses, dils, *, heads):
    bsz, seq, hw = outs[0].shape
    groups = len(outs)
    assert DSA_HEAD_DIM == V7X_LANES
    out = pl.pallas_call(
        functools.partial(_merge_kernel, dils=tuple(dils), heads=heads),
        out_shape=jax.ShapeDtypeStruct((bsz, seq, hw), BF16),
        grid=(bsz,),
        in_specs=[pl.BlockSpec((1, seq, hw), lambda b: (b, 0, 0))] * groups
        + [pl.BlockSpec((1, seq, V7X_LANES), lambda b: (b, 0, 0))] * groups,
        out_specs=pl.BlockSpec((1, seq, hw), lambda b: (b, 0, 0)),
        scratch_shapes=[pltpu.VMEM((groups, seq, V7X_LANES), F32),
                        pltpu.VMEM((groups, seq, V7X_LANES), F32)],
        compiler_params=_params("parallel"),
        name="merge_groups",
    )(*outs, *lses)
    return out.reshape(bsz * seq, hw)


def _mla_attn_kernel(q_ref, kv_ref, kpe_ref, o_ref, *, bk):
    qi = pl.program_id(2)
    bq = q_ref.shape[1]
    steps = bq // bk
    rq = lax.broadcasted_iota(jnp.int32, (bq, bk), 0)
    ck = lax.broadcasted_iota(jnp.int32, (bq, bk), 1)

    def scores(q, kb):
        rows = slice(kb * bk, (kb + 1) * bk)
        k = jnp.concatenate([kv_ref[0, rows, :MLA_NOPE],
                             kpe_ref[0, rows, :].astype(BF16)], axis=1)
        return lax.dot_general(q, k, (((1,), (1,)), ((), ())), preferred_element_type=F32)

    def update(m_i, l_i, acc, s, kb):
        v = kv_ref[0, kb * bk:(kb + 1) * bk, MLA_NOPE:]
        m_new = jnp.maximum(m_i, jnp.max(s, axis=-1, keepdims=True))
        alpha = jnp.exp2(m_i - m_new)
        p = jnp.exp2(s - m_new)
        l_new = alpha * l_i + jnp.sum(p, axis=-1, keepdims=True)
        acc = alpha * acc + jnp.dot(p.astype(BF16), v, preferred_element_type=F32)
        return m_new, l_new, acc

    def attend(n_full):
        q = q_ref[0]
        n_blocks = n_full + steps
        m_i = jnp.full((bq, 1), NEG_INF, F32)
        l_i = jnp.zeros((bq, 1), F32)
        acc = jnp.zeros((bq, MLA_V), F32)
        s = scores(q, 0)
        for kb in range(n_blocks):
            s_next = scores(q, kb + 1) if kb + 1 < n_blocks else None
            if kb >= n_full:
                s = jnp.where(ck + (kb - n_full) * bk <= rq, s, NEG_INF)
            m_i, l_i, acc = update(m_i, l_i, acc, s, kb)
            s = s_next
        o_ref[0] = (acc / l_i).astype(o_ref.dtype)

    for blk in range(kv_ref.shape[1] // bq):
        pl.when(qi == blk)(functools.partial(attend, blk * steps))


def mla_attention(q, kv, c, *, bsz, seq, kpe_block):
    hq = MLA_NOPE + V7X_LANES
    heads = q.shape[1] // hq
    bq = _tile(seq, MLA_BQ)
    bk = _tile(bq, MLA_BK)
    q3 = q.reshape(bsz, seq, heads * hq)
    kv3 = kv.reshape(bsz, seq, kv.shape[1])
    c3 = c.reshape(bsz, seq, c.shape[1])
    o = pl.pallas_call(
        functools.partial(_mla_attn_kernel, bk=bk),
        out_shape=jax.ShapeDtypeStruct((bsz, seq, heads * MLA_V), BF16),
        grid=(bsz, heads, seq // bq),
        in_specs=[pl.BlockSpec((1, bq, hq), lambda b, h, i: (b, i, h)),
                  pl.BlockSpec((1, seq, MLA_NOPE + MLA_V), lambda b, h, i: (b, 0, h)),
                  pl.BlockSpec((1, seq, V7X_LANES), lambda b, h, i: (b, 0, kpe_block))],
        out_specs=pl.BlockSpec((1, bq, MLA_V), lambda b, h, i: (b, i, h)),
        compiler_params=_params("parallel", "parallel", "parallel"),
        name="mla_attn",
    )(q3, kv3, c3)
    return o.reshape(bsz * seq, heads * MLA_V)


def _sgu_out_kernel(u_ref, v_ref, st_ref, lng_ref, lnb_ref, ws_ref, bs_ref, wo_ref, bo_ref,
                    h_ref, o_ref, acc_ref, gate_ref, *, width):
    g = pl.program_id(1)
    ng = pl.num_programs(1)
    bm = u_ref.shape[0]
    ch = ws_ref.shape[1]
    st = st_ref[...]
    mu = st[:, 0:1] * (1.0 / width)
    var = st[:, 1:2] * (1.0 / width) - mu * mu
    rstd = lax.rsqrt(var + NORM_EPS)
    ti = lax.broadcasted_iota(jnp.int32, (ch, ch), 0)
    si = lax.broadcasted_iota(jnp.int32, (ch, ch), 1)
    ws = jnp.where(si <= ti, ws_ref[0], 0.0).astype(BF16)
    bs = bs_ref[0]
    lng = lng_ref[...]
    lnb = lnb_ref[...]
    for c in range(bm // ch):
        rows = slice(c * ch, (c + 1) * ch)
        vn = (v_ref[rows, :].astype(F32) - mu[rows]) * rstd[rows] * lng + lnb
        mixed = jnp.dot(ws, vn.astype(BF16), preferred_element_type=F32) + bs
        gate_ref[rows, :] = (u_ref[rows, :].astype(F32) * mixed).astype(BF16)
    part = jnp.dot(gate_ref[...], wo_ref[...], preferred_element_type=F32)

    @pl.when(g == 0)
    def _():
        acc_ref[...] = part

    @pl.when(g > 0)
    def _():
        acc_ref[...] += part

    @pl.when(g == ng - 1)
    def _():
        o_ref[...] = acc_ref[...] + bo_ref[...] + h_ref[...]


def sgu_out(u, v, st, ln_g, ln_b, w_s, b_s, w_out, b_out, h):
    m, e = u.shape
    d = w_out.shape[1]
    groups, ch, _ = w_s.shape
    gw = e // groups
    bm = _tile(m, SGU_BM)
    assert bm % ch == 0
    return pl.pallas_call(
        functools.partial(_sgu_out_kernel, width=e),
        out_shape=jax.ShapeDtypeStruct((m, d), F32),
        grid=(m // bm, groups),
        in_specs=[
            pl.BlockSpec((bm, gw), lambda i, g: (i, g)),
            pl.BlockSpec((bm, gw), lambda i, g: (i, g)),
            pl.BlockSpec((bm, V7X_LANES), lambda i, g: (i, 0)),
            pl.BlockSpec((1, gw), lambda i, g: (0, g)),
            pl.BlockSpec((1, gw), lambda i, g: (0, g)),
            pl.BlockSpec((1, ch, ch), lambda i, g: (g, 0, 0)),
            pl.BlockSpec((1, ch, 1), lambda i, g: (g, 0, 0)),
            pl.BlockSpec((gw, d), lambda i, g: (g, 0)),
            pl.BlockSpec((1, d), lambda i, g: (0, 0)),
            pl.BlockSpec((bm, d), lambda i, g: (i, 0)),
        ],
        out_specs=pl.BlockSpec((bm, d), lambda i, g: (i, 0)),
        scratch_shapes=[pltpu.VMEM((bm, d), F32), pltpu.VMEM((bm, gw), BF16)],
        compiler_params=_params("parallel", "arbitrary"),
        name="sgu_out",
    )(u, v, st, ln_g.reshape(1, e).astype(F32), ln_b.reshape(1, e).astype(F32),
      w_s.astype(F32), b_s.reshape(groups, ch, 1).astype(F32), w_out.astype(BF16),
      b_out.reshape(1, d).astype(F32), h)


def _rope_tables(seq, dim):
    half = dim // 2
    pos = jnp.arange(seq, dtype=F32)
    inv = ROPE_THETA ** (-(jnp.arange(0, dim, 2, dtype=F32) / dim))
    ang = pos[:, None] * inv[None, :]
    zeros = jnp.zeros((seq, V7X_LANES // 2 - half), F32)
    cos = jnp.concatenate([jnp.cos(ang), zeros, jnp.cos(ang), zeros], axis=1)
    sin = jnp.concatenate([-jnp.sin(ang), zeros, jnp.sin(ang), zeros], axis=1)
    return cos, sin


def _spread_rope_cols(w):
    k, dim = w.shape
    half = dim // 2
    zeros = jnp.zeros((k, V7X_LANES // 2 - half), w.dtype)
    return jnp.concatenate([w[:, :half], zeros, w[:, half:], zeros], axis=1)


def conformer_mixer(xn, h, w_in, b_in, dw_w, dw_b, ln_g, ln_b, w_out, b_out, *, bsz, seq):
    c = w_out.shape[0]
    wi = w_in.astype(BF16)
    z = matmul(xn, wi[:, :c], w2=wi[:, c:], bias=b_in[:c], bias2=b_in[c:])
    z = causal_dwconv(z.reshape(bsz, seq, c), dw_w, dw_b).reshape(bsz * seq, c)
    return matmul(z, w_out.astype(BF16), prologue="ln_silu", pg=ln_g, pb=ln_b, bias=b_out,
                  residual=h, out_dtype=F32)


def dilated_attention_mixer(xn, h, w_qkv, w_o, *, bsz, seq):
    d = xn.shape[1]
    dh = DSA_HEAD_DIM
    heads = w_o.shape[0] // dh
    groups = len(DSA_CONFIGS)
    hw = heads * dh
    w5 = w_qkv.reshape(d, groups, 3, hw)
    scale = jnp.array([dh ** -0.5, 1.0, 1.0], F32).reshape(1, 1, 3, 1)
    w5 = (w5 * scale).astype(BF16)
    tables = _rope_tables(seq, dh)
    outs, lses = [], []
    for g, (window, dil) in enumerate(DSA_CONFIGS):
        qkv = dsa_project(xn, w5[:, g].reshape(d, 3 * hw), tables, dil, bsz=bsz, seq=seq,
                          rope_cols=2 * hw)
        o, lse = band_attention(qkv, window, dil, bsz=bsz, seq=seq, heads=heads)
        outs.append(o)
        lses.append(lse)
    o = merge_groups(outs, lses, [dil for _, dil in DSA_CONFIGS], heads=heads)
    return matmul(o, w_o.astype(BF16), residual=h, out_dtype=F32)


def mla_mixer(xn, h, w_in, q_norm, w_qb, kv_norm, w_kvb, w_o, *, bsz, seq):
    q_rank = q_norm.shape[0]
    kv_rank = kv_norm.shape[0]
    assert q_rank == kv_rank
    heads = MLA_HEADS
    w_in_p = jnp.concatenate(
        [w_in[:, :q_rank + kv_rank], _spread_rope_cols(w_in[:, q_rank + kv_rank:])],
        axis=1).astype(BF16)
    n_c = w_in_p.shape[1]
    kpe_group = (q_rank + kv_rank) // V7X_LANES
    cmask = tuple(k == kpe_group for k in range(n_c // V7X_LANES))
    tables = _rope_tables(seq, MLA_ROPE)
    c = matmul(xn, w_in_p, rope=tables, rope_mask=cmask, seq=seq, out_dtype=F32, bn=n_c)
    q_scale = (MLA_NOPE + MLA_ROPE) ** -0.5 * math.log2(math.e)
    wq3 = w_qb.reshape(q_rank, heads, MLA_NOPE + MLA_ROPE) * q_scale
    wq_pe = jax.vmap(_spread_rope_cols, in_axes=1, out_axes=1)(wq3[:, :, MLA_NOPE:])
    wq_p = jnp.concatenate([wq3[:, :, :MLA_NOPE], wq_pe], axis=2)
    wq_p = wq_p.reshape(q_rank, heads * (MLA_NOPE + V7X_LANES)).astype(BF16)
    bn = _tile(wq_p.shape[1], MLA_PROJ_BN)
    qmask = tuple(k % 2 == 1 for k in range(bn // V7X_LANES))
    q = matmul(c, wq_p, x_cols=(0, q_rank), prologue="rms", pg=q_norm, rope=tables,
               rope_mask=qmask, seq=seq, bm=MM_BM, bn=bn)
    kv = matmul(c, w_kvb.astype(BF16), x_cols=(1, kv_rank), prologue="rms", pg=kv_norm,
                bm=MM_BM, bn=MLA_PROJ_BN)
    o = mla_attention(q, kv, c, bsz=bsz, seq=seq, kpe_block=kpe_group)
    return matmul(o, w_o.astype(BF16), residual=h, out_dtype=F32)


def sgu_mixer(xn, h, w_in, b_in, ln_g, ln_b, w_s, b_s, w_out, b_out):
    e = w_out.shape[0]
    wi = w_in.astype(BF16)
    u = matmul(xn, wi[:, :e], bias=b_in[:e], gelu=True)
    v, st = matmul(xn, wi[:, e:], bias=b_in[e:], gelu=True, stats=True)
    return sgu_out(u, v, st, ln_g, ln_b, w_s, b_s, w_out, b_out, h)


def kernel(x, l0_norm_mix, l0_cc_w_in, l0_cc_b_in, l0_cc_dw_w, l0_cc_dw_b, l0_cc_ln_g, l0_cc_ln_b, l0_cc_w_out, l0_cc_b_out, l0_norm_ffn, l0_ffn_w_up, l0_ffn_dw_w, l0_ffn_dw_b, l0_ffn_w_down, l1_norm_mix, l1_dsa_w_qkv, l1_dsa_w_o, l1_norm_ffn, l1_ffn_w_up, l1_ffn_dw_w, l1_ffn_dw_b, l1_ffn_w_down, l2_norm_mix, l2_mla_w_in, l2_mla_q_norm, l2_mla_w_qb, l2_mla_kv_norm, l2_mla_w_kvb, l2_mla_w_o, l2_norm_ffn, l2_ffn_w_up, l2_ffn_dw_w, l2_ffn_dw_b, l2_ffn_w_down, l3_norm_mix, l3_sg_w_in, l3_sg_b_in, l3_sg_ln_g, l3_sg_ln_b, l3_sg_w_s, l3_sg_b_s, l3_sg_w_out, l3_sg_b_out, l3_norm_ffn, l3_ffn_w_up, l3_ffn_dw_w, l3_ffn_dw_b, l3_ffn_w_down, final_norm):
    bsz, seq, d = x.shape
    dims = dict(bsz=bsz, seq=seq)
    h = x.reshape(bsz * seq, d)

    def ffn(h, norm, w_up, dw_w, dw_b, w_down, **tail):
        return conv_ffn(h, norm, w_up, dw_w, dw_b, w_down, seq=seq, **tail)

    feeds_next = dict(tail="residual_and_norm")
    h = conformer_mixer(rmsnorm(h, l0_norm_mix, BF16), h, l0_cc_w_in, l0_cc_b_in, l0_cc_dw_w,
                        l0_cc_dw_b, l0_cc_ln_g, l0_cc_ln_b, l0_cc_w_out, l0_cc_b_out, **dims)
    h, xn = ffn(h, l0_norm_ffn, l0_ffn_w_up, l0_ffn_dw_w, l0_ffn_dw_b, l0_ffn_w_down,
                tail_g=l1_norm_mix, **feeds_next)
    h = dilated_attention_mixer(xn, h, l1_dsa_w_qkv, l1_dsa_w_o, **dims)
    h, xn = ffn(h, l1_norm_ffn, l1_ffn_w_up, l1_ffn_dw_w, l1_ffn_dw_b, l1_ffn_w_down,
                tail_g=l2_norm_mix, **feeds_next)
    h = mla_mixer(xn, h, l2_mla_w_in, l2_mla_q_norm, l2_mla_w_qb, l2_mla_kv_norm, l2_mla_w_kvb,
                  l2_mla_w_o, **dims)
    h, xn = ffn(h, l2_norm_ffn, l2_ffn_w_up, l2_ffn_dw_w, l2_ffn_dw_b, l2_ffn_w_down,
                tail_g=l3_norm_mix, **feeds_next)
    h = sgu_mixer(xn, h, l3_sg_w_in, l3_sg_b_in, l3_sg_ln_g, l3_sg_ln_b, l3_sg_w_s, l3_sg_b_s,
                  l3_sg_w_out, l3_sg_b_out)
    out = ffn(h, l3_norm_ffn, l3_ffn_w_up, l3_ffn_dw_w, l3_ffn_dw_b, l3_ffn_w_down,
              tail="norm_only", tail_g=final_norm)
    return out.reshape(bsz, seq, d)
```

```python
import functools
import math

import jax
import jax.numpy as jnp
from jax import lax
from jax.experimental import pallas as pl
from jax.experimental.pallas import tpu as pltpu

F32 = jnp.float32
BF16 = jnp.bfloat16

NORM_EPS = 1e-6
ROPE_THETA = 10000.0
NEG_INF = -1e30

DSA_HEAD_DIM = 128
DSA_CONFIGS = ((128, 1), (512, 4), (2048, 16))
DSA_BLOCK = 128
MLA_HEADS = 16
MLA_NOPE = 128
MLA_ROPE = 64
MLA_V = 128
SG_CHUNK = 128
SG_GROUPS = 8

V7X_LANES = 128
V7X_SUBLANES = 8
V7X_VMEM_LIMIT_BYTES = 56 * 1024 * 1024

MM_BM, MM_BN = 1024, 1024
MM_PROLOGUE_BM = 1024
MM_PROLOGUE_ROWS = 128
FFN_BM, FFN_FC = 512, 512
FFN_SUB, FFN_RSUB = 512, 256
MLA_BQ, MLA_BK = 512, 512
SGU_BM = 512
CONV_CB, CONV_ROWS = 256, 64
DSA_PROJ_BN = 512
MLA_PROJ_BN = 2048


def _params(*semantics):
    return pltpu.CompilerParams(dimension_semantics=semantics,
                                vmem_limit_bytes=V7X_VMEM_LIMIT_BYTES)


def _sigmoid(x):
    return 1.0 / (1.0 + jnp.exp(-x))


def _tile(n, want):
    t = min(n, want)
    while n % t:
        t -= 1
    return t


def _rmsnorm_kernel(x_ref, g_ref, o_ref):
    x = x_ref[...].astype(F32)
    y = x * lax.rsqrt(jnp.mean(x * x, axis=-1, keepdims=True) + NORM_EPS)
    o_ref[...] = (y * g_ref[...]).astype(o_ref.dtype)


def rmsnorm(x, g, out_dtype):
    m, d = x.shape
    bm = _tile(m, 512)
    return pl.pallas_call(
        _rmsnorm_kernel,
        out_shape=jax.ShapeDtypeStruct((m, d), out_dtype),
        grid=(m // bm,),
        in_specs=[pl.BlockSpec((bm, d), lambda i: (i, 0)),
                  pl.BlockSpec((1, d), lambda i: (0, 0))],
        out_specs=pl.BlockSpec((bm, d), lambda i: (i, 0)),
        compiler_params=_params("parallel"),
        name="rmsnorm",
    )(x, g.reshape(1, d).astype(F32))


def _mm_kernel(*refs, prologue, has_bias, glu, gelu, rope_mask, residual, stats):
    it = iter(refs)
    x_ref = next(it)
    pg_ref = next(it) if prologue else None
    pb_ref = next(it) if prologue == "ln_silu" else None
    w_ref = next(it)
    w2_ref = next(it) if glu else None
    b_ref = next(it) if has_bias else None
    b2_ref = next(it) if (glu and has_bias) else None
    cos_ref = next(it) if rope_mask else None
    sin_ref = next(it) if rope_mask else None
    res_ref = next(it) if residual else None
    o_ref = next(it)
    st_ref = next(it) if stats else None
    xs_ref = next(it) if prologue else None

    j = pl.program_id(1)

    if prologue:
        @pl.when(j == 0)
        def _():
            chunk = _tile(x_ref.shape[0], MM_PROLOGUE_ROWS)
            for r0 in range(0, x_ref.shape[0], chunk):
                xf = x_ref[r0:r0 + chunk, :].astype(F32)
                if prologue == "rms":
                    y = xf * lax.rsqrt(jnp.mean(xf * xf, axis=-1, keepdims=True) + NORM_EPS)
                    y = y * pg_ref[...]
                else:
                    mu = jnp.mean(xf, axis=-1, keepdims=True)
                    xc = xf - mu
                    y = xc * lax.rsqrt(jnp.mean(xc * xc, axis=-1, keepdims=True) + NORM_EPS)
                    y = y * pg_ref[...] + pb_ref[...]
                    y = y * _sigmoid(y)
                xs_ref[r0:r0 + chunk, :] = y.astype(BF16)
        xv = xs_ref[...]
    else:
        xv = x_ref[...]

    acc = jnp.dot(xv, w_ref[...], preferred_element_type=F32)
    if has_bias:
        acc = acc + b_ref[...]
    if glu:
        gate = jnp.dot(xv, w2_ref[...], preferred_element_type=F32)
        if has_bias:
            gate = gate + b2_ref[...]
        acc = acc * _sigmoid(gate)
    if gelu:
        acc = 0.5 * acc * (1.0 + lax.erf(acc * (2.0 ** -0.5)))
    if stats:
        s1 = jnp.sum(acc, axis=-1, keepdims=True)
        s2 = jnp.sum(acc * acc, axis=-1, keepdims=True)
        lane = lax.broadcasted_iota(jnp.int32, st_ref.shape, 1)
        upd = jnp.where(lane == 0, s1, jnp.where(lane == 1, s2, 0.0))

        @pl.when(j == 0)
        def _():
            st_ref[...] = upd

        @pl.when(j > 0)
        def _():
            st_ref[...] += upd
    if residual:
        acc = acc + res_ref[...]
    if rope_mask:
        cos = cos_ref[...]
        sin = sin_ref[...]
        for k, on in enumerate(rope_mask):
            sl = acc[:, k * V7X_LANES:(k + 1) * V7X_LANES]
            if on:
                sl = sl * cos + pltpu.roll(sl, V7X_LANES // 2, 1) * sin
            o_ref[:, k * V7X_LANES:(k + 1) * V7X_LANES] = sl.astype(o_ref.dtype)
    else:
        o_ref[...] = acc.astype(o_ref.dtype)


def matmul(x, w, *, x_cols=None, prologue=None, pg=None, pb=None, w2=None, bias=None,
           bias2=None, gelu=False, rope=None, rope_mask=None, seq=None, residual=None,
           stats=False, out_dtype=BF16, bm=None, bn=None):
    m = x.shape[0]
    k, n = w.shape
    kblk = 0 if x_cols is None else x_cols[0]
    bm = _tile(m, bm or (MM_PROLOGUE_BM if prologue else MM_BM))
    bn = _tile(n, bn or MM_BN)
    glu = w2 is not None
    has_bias = bias is not None
    if rope_mask:
        assert bn % V7X_LANES == 0 and len(rope_mask) == bn // V7X_LANES
        assert seq % bm == 0
    args, specs = [x], [pl.BlockSpec((bm, k), lambda i, j: (i, kblk))]
    if prologue:
        args.append(pg.reshape(1, k).astype(F32))
        specs.append(pl.BlockSpec((1, k), lambda i, j: (0, 0)))
        if prologue == "ln_silu":
            args.append(pb.reshape(1, k).astype(F32))
            specs.append(pl.BlockSpec((1, k), lambda i, j: (0, 0)))
    args.append(w)
    specs.append(pl.BlockSpec((k, bn), lambda i, j: (0, j)))
    if glu:
        args.append(w2)
        specs.append(pl.BlockSpec((k, bn), lambda i, j: (0, j)))
    if has_bias:
        args.append(bias.reshape(1, n).astype(F32))
        specs.append(pl.BlockSpec((1, bn), lambda i, j: (0, j)))
        if glu:
            args.append(bias2.reshape(1, n).astype(F32))
            specs.append(pl.BlockSpec((1, bn), lambda i, j: (0, j)))
    if rope_mask:
        tiles_per_seq = seq // bm
        for t in rope:
            args.append(t)
            specs.append(pl.BlockSpec((bm, V7X_LANES), lambda i, j: (i % tiles_per_seq, 0)))
    if residual is not None:
        args.append(residual)
        specs.append(pl.BlockSpec((bm, bn), lambda i, j: (i, j)))
    out_shape = [jax.ShapeDtypeStruct((m, n), out_dtype)]
    out_specs = [pl.BlockSpec((bm, bn), lambda i, j: (i, j))]
    if stats:
        out_shape.append(jax.ShapeDtypeStruct((m, V7X_LANES), F32))
        out_specs.append(pl.BlockSpec((bm, V7X_LANES), lambda i, j: (i, 0)))
    scratch = [pltpu.VMEM((bm, k), BF16)] if prologue else []
    kern = functools.partial(
        _mm_kernel, prologue=prologue, has_bias=has_bias, glu=glu, gelu=gelu,
        rope_mask=tuple(rope_mask) if rope_mask else None,
        residual=residual is not None, stats=stats)
    out = pl.pallas_call(
        kern,
        out_shape=out_shape,
        grid=(m // bm, n // bn),
        in_specs=specs,
        out_specs=out_specs,
        scratch_shapes=scratch,
        compiler_params=_params("parallel", "arbitrary"),
        name="matmul",
    )(*args)
    return out if stats else out[0]


def _dwconv_kernel(x_ref, w_ref, b_ref, o_ref, sh_ref, *, width, row_chunk):
    s, cb = x_ref.shape[1], x_ref.shape[2]
    pad = sh_ref.shape[1] - s
    x = x_ref[0].astype(F32)
    xz = jnp.concatenate([x, jnp.zeros((V7X_SUBLANES, cb), F32)], axis=0)
    zero_top = jnp.zeros((pad, cb), F32)
    for r in range(V7X_SUBLANES):
        sh_ref[r, :pad, :] = zero_top
        shifted = xz if r == 0 else pltpu.roll(xz, r, 0)
        sh_ref[r, pad:, :] = shifted[:s]

    w = w_ref[...]
    bias = b_ref[...]

    def chunk(c, carry):
        base = pl.multiple_of(c * row_chunk, row_chunk)
        acc = jnp.zeros((row_chunk, cb), F32) + bias
        for kk in range(width):
            shift = width - 1 - kk
            q, r = divmod(shift, V7X_SUBLANES)
            start = pl.multiple_of(base + (pad - q * V7X_SUBLANES), V7X_SUBLANES)
            tap = sh_ref[r, pl.ds(start, row_chunk), :]
            acc = acc + tap * w[kk:kk + 1, :]
        o_ref[0, pl.ds(base, row_chunk), :] = acc.astype(o_ref.dtype)
        return carry

    lax.fori_loop(0, s // row_chunk, chunk, 0)


def causal_dwconv(x, w, b):
    bsz, s, c = x.shape
    width = w.shape[0]
    cb = _tile(c, CONV_CB)
    row_chunk = _tile(s, CONV_ROWS)
    pad = -(-(width - 1) // V7X_SUBLANES) * V7X_SUBLANES
    return pl.pallas_call(
        functools.partial(_dwconv_kernel, width=width, row_chunk=row_chunk),
        out_shape=jax.ShapeDtypeStruct((bsz, s, c), BF16),
        grid=(bsz, c // cb),
        in_specs=[pl.BlockSpec((1, s, cb), lambda i, j: (i, 0, j)),
                  pl.BlockSpec((width, cb), lambda i, j: (0, j)),
                  pl.BlockSpec((1, cb), lambda i, j: (0, j))],
        out_specs=pl.BlockSpec((1, s, cb), lambda i, j: (i, 0, j)),
        scratch_shapes=[pltpu.VMEM((V7X_SUBLANES, pad + s, cb), F32)],
        compiler_params=_params("parallel", "parallel"),
        name="dwconv",
    )(x, w.astype(F32), b.reshape(1, c).astype(F32))


def _ffn_kernel(h_ref, ng_ref, og_ref, wg_ref, wa_ref, cwg_ref, cwa_ref, cbg_ref, cba_ref,
                wd_ref, o_ref, *rest, tiles_per_seq, nf, sub, rsub, tail):
    xn_ref = rest[0] if tail == "residual_and_norm" else None
    xs_ref, acc_ref, carry_g_ref, carry_a_ref = rest[-4:]
    i = pl.program_id(0)
    j = pl.program_id(1)
    bm, d = h_ref.shape
    fc = wg_ref.shape[1]
    seq_start = (i % tiles_per_seq) == 0
    sl = V7X_SUBLANES
    nv = rsub // sl
    slabs = [slice(c * V7X_LANES, (c + 1) * V7X_LANES) for c in range(d // V7X_LANES)]

    na = nv // sl

    def staged(r, a, k):
        return pl.ds(r * rsub + sl * sl * a + k, sl, stride=sl)

    @pl.when(seq_start)
    def _():
        carry_g_ref[j] = jnp.zeros((2 * sl, fc), F32)
        carry_a_ref[j] = jnp.zeros((2 * sl, fc), F32)

    def normalise(r):
        gain = ng_ref[...]
        for s in range(sl):
            for a in range(na):
                src = r * rsub + nv * s + sl * a
                dst = r * rsub + sl * sl * a + sl * s
                hv = h_ref[src:src + sl, :]
                y = hv * lax.rsqrt(jnp.mean(hv * hv, axis=-1, keepdims=True) + NORM_EPS) * gain
                for c, cols in enumerate(slabs):
                    acc_ref[c, dst:dst + sl, :] = y[:, cols]
        for a in range(na):
            for b in range(0, sl, 2):
                dst = r * rsub + sl * (sl * a + b)
                for c, cols in enumerate(slabs):
                    pair = jnp.concatenate([acc_ref[c, staged(r, a, b), :],
                                            acc_ref[c, staged(r, a, b + 1), :]], axis=0)
                    xs_ref[dst:dst + 2 * sl, cols] = pair.astype(BF16)
        for c in range(len(slabs)):
            acc_ref[c, r * rsub:(r + 1) * rsub, :] = jnp.zeros((rsub, V7X_LANES), F32)

    def finish(r):
        for s in range(sl):
            for a in range(0, na, 2):
                dst = r * rsub + nv * s + sl * a
                rows = slice(dst, dst + 2 * sl)
                vals = [jnp.concatenate([acc_ref[c, staged(r, a, s), :],
                                         acc_ref[c, staged(r, a + 1, s), :]], axis=0)
                        + h_ref[rows, cols] for c, cols in enumerate(slabs)]
                if tail != "residual":
                    ss = functools.reduce(
                        lambda p, q: p + q,
                        [jnp.sum(v * v, axis=-1, keepdims=True) for v in vals])
                    inv = lax.rsqrt(ss * (1.0 / d) + NORM_EPS)
                    normed = [v * inv * og_ref[:, cols] for v, cols in zip(vals, slabs)]
                if tail == "norm_only":
                    vals = normed
                for v, cols in zip(vals, slabs):
                    o_ref[rows, cols] = v
                if tail == "residual_and_norm":
                    for v, cols in zip(normed, slabs):
                        xn_ref[rows, cols] = v.astype(BF16)

    first_row = lax.broadcasted_iota(jnp.int32, (sl, sub), 0) == 0

    def shift1(z, prev_row):
        top = jnp.where(first_row, prev_row, pltpu.roll(z[rsub - sl:, :], 1, 0))
        return jnp.concatenate([top, z[:rsub - sl, :]], axis=0)

    def conv(z, prev, cw, cb):
        z1 = shift1(z, prev[2 * sl - 1:2 * sl, :])
        z2 = shift1(z1, prev[sl - 1:sl, :])
        return z2 * cw[0:1, :] + z1 * cw[1:2, :] + z * cw[2:3, :] + cb

    n_r = bm // rsub
    tiles = [(r, slice(c * sub, (c + 1) * sub)) for c in range(fc // sub) for r in range(n_r)]

    def up(r, cols):
        xr = xs_ref[r * rsub:(r + 1) * rsub, :]
        return (jnp.dot(xr, wg_ref[:, cols], preferred_element_type=F32),
                jnp.dot(xr, wa_ref[:, cols], preferred_element_type=F32))

    def step(first, last):
        if first:
            for r in range(n_r):
                normalise(r)
        z_next = up(*tiles[0])
        for k, (r, cols) in enumerate(tiles):
            rows = slice(r * rsub, (r + 1) * rsub)
            zg, za = z_next
            if k + 1 < len(tiles):
                z_next = up(*tiles[k + 1])
            if r == 0:
                prev_g = carry_g_ref[j, :, cols]
                prev_a = carry_a_ref[j, :, cols]
            g = conv(zg, prev_g, cwg_ref[:, cols], cbg_ref[:, cols])
            a = conv(za, prev_a, cwa_ref[:, cols], cba_ref[:, cols])
            prev_g = zg[rsub - 2 * sl:, :]
            prev_a = za[rsub - 2 * sl:, :]
            if r == n_r - 1:
                carry_g_ref[j, :, cols] = prev_g
                carry_a_ref[j, :, cols] = prev_a
            hg = 0.5 * g
            act = ((hg + hg * jnp.tanh(hg)) * a).astype(BF16)
            part = jnp.dot(act, wd_ref[cols, :], preferred_element_type=F32)
            for c, dcols in enumerate(slabs):
                acc_ref[c, rows, :] += part[:, dcols]
            if last and k >= len(tiles) - n_r:
                finish(r)

    if nf == 1:
        step(True, True)
    else:
        pl.when(j == 0)(functools.partial(step, True, False))
        pl.when((j > 0) & (j < nf - 1))(functools.partial(step, False, False))
        pl.when(j == nf - 1)(functools.partial(step, False, True))


def conv_ffn(h, norm_g, w_up, dw_w, dw_b, w_down, *, seq, tail="residual", tail_g=None):
    m, d = h.shape
    f = w_down.shape[0]
    bm = _tile(seq, FFN_BM)
    fc = _tile(f, FFN_FC)
    nf = f // fc
    wu = w_up.astype(BF16)
    wd = w_down.astype(BF16)
    cw = dw_w.astype(F32)
    cb = dw_b.reshape(1, 2 * f).astype(F32)
    rsub = _tile(bm, FFN_RSUB)
    assert rsub % (2 * V7X_SUBLANES * V7X_SUBLANES) == 0 and d % V7X_LANES == 0
    kern = functools.partial(_ffn_kernel, tiles_per_seq=seq // bm, nf=nf,
                             sub=_tile(fc, FFN_SUB), rsub=rsub, tail=tail)
    gains = [g.reshape(1, d).astype(F32) for g in (norm_g, norm_g if tail_g is None else tail_g)]
    row_tile = pl.BlockSpec((bm, d), lambda i, j: (i, 0))
    out_shape = [jax.ShapeDtypeStruct((m, d), F32)]
    if tail == "residual_and_norm":
        out_shape.append(jax.ShapeDtypeStruct((m, d), BF16))
    out = pl.pallas_call(
        kern,
        out_shape=out_shape,
        grid=(m // bm, nf),
        in_specs=[
            row_tile,
            pl.BlockSpec((1, d), lambda i, j: (0, 0)),
            pl.BlockSpec((1, d), lambda i, j: (0, 0)),
            pl.BlockSpec((d, fc), lambda i, j: (0, j)),
            pl.BlockSpec((d, fc), lambda i, j: (0, j + nf)),
            pl.BlockSpec((3, fc), lambda i, j: (0, j)),
            pl.BlockSpec((3, fc), lambda i, j: (0, j + nf)),
            pl.BlockSpec((1, fc), lambda i, j: (0, j)),
            pl.BlockSpec((1, fc), lambda i, j: (0, j + nf)),
            pl.BlockSpec((fc, d), lambda i, j: (j, 0)),
        ],
        out_specs=[row_tile] * len(out_shape),
        scratch_shapes=[
            pltpu.VMEM((bm, d), BF16),
            pltpu.VMEM((d // V7X_LANES, bm, V7X_LANES), F32),
            pltpu.VMEM((nf, 2 * V7X_SUBLANES, fc), F32),
            pltpu.VMEM((nf, 2 * V7X_SUBLANES, fc), F32),
        ],
        compiler_params=_params("arbitrary", "arbitrary"),
        name="conv_ffn",
    )(h, *gains, wu, wu, cw, cw, cb, cb, wd)
    return out if tail == "residual_and_norm" else out[0]


def _band_attn_kernel(q_ref, kp_ref, kc_ref, vp_ref, vc_ref, o_ref, lse_ref, *, heads, span):
    n = pl.program_id(2)
    blk = q_ref.shape[1]
    qi = lax.broadcasted_iota(jnp.int32, (blk, 2 * blk), 0)
    kj = lax.broadcasted_iota(jnp.int32, (blk, 2 * blk), 1)
    dist = blk + qi - kj
    first_key = jnp.where(n > 0, 0, blk)
    mask = (dist >= 0) & (dist <= span) & (kj >= first_key)
    lane = lax.broadcasted_iota(jnp.int32, (blk, V7X_LANES), 1)
    lse_tile = jnp.zeros((blk, V7X_LANES), F32)
    dh = DSA_HEAD_DIM
    for hh in range(heads):
        cols = slice(hh * dh, (hh + 1) * dh)
        q = q_ref[0, :, cols]
        k = jnp.concatenate([kp_ref[0, :, cols], kc_ref[0, :, cols]], axis=0)
        v = jnp.concatenate([vp_ref[0, :, cols], vc_ref[0, :, cols]], axis=0)
        s = lax.dot_general(q, k, (((1,), (1,)), ((), ())), preferred_element_type=F32)
        s = jnp.where(mask, s, NEG_INF)
        mx = jnp.max(s, axis=-1, keepdims=True)
        p = jnp.exp(s - mx)
        l = jnp.sum(p, axis=-1, keepdims=True)
        o = jnp.dot(p.astype(BF16), v, preferred_element_type=F32) / l
        o_ref[0, :, cols] = o.astype(o_ref.dtype)
        lse_tile = jnp.where(lane == hh, mx + jnp.log(l), lse_tile)
    lse_ref[0] = lse_tile


def band_attention(qkv, window, dil, *, bsz, seq, heads):
    hw = heads * DSA_HEAD_DIM
    blk = DSA_BLOCK
    nb = seq // dil // blk
    span = window // dil
    qkv3 = qkv.reshape(bsz, seq, 3 * hw)
    cur = lambda r, n: r * nb + n
    prev = lambda r, n: r * nb + jnp.maximum(n - 1, 0)
    return pl.pallas_call(
        functools.partial(_band_attn_kernel, heads=heads, span=span),
        out_shape=[jax.ShapeDtypeStruct((bsz, seq, hw), BF16),
                   jax.ShapeDtypeStruct((bsz, seq, V7X_LANES), F32)],
        grid=(bsz, dil, nb),
        in_specs=[
            pl.BlockSpec((1, blk, hw), lambda b, r, n: (b, cur(r, n), 0)),
            pl.BlockSpec((1, blk, hw), lambda b, r, n: (b, prev(r, n), 1)),
            pl.BlockSpec((1, blk, hw), lambda b, r, n: (b, cur(r, n), 1)),
            pl.BlockSpec((1, blk, hw), lambda b, r, n: (b, prev(r, n), 2)),
            pl.BlockSpec((1, blk, hw), lambda b, r, n: (b, cur(r, n), 2)),
        ],
        out_specs=[pl.BlockSpec((1, blk, hw), lambda b, r, n: (b, cur(r, n), 0)),
                   pl.BlockSpec((1, blk, V7X_LANES), lambda b, r, n: (b, cur(r, n), 0))],
        compiler_params=_params("parallel", "parallel", "parallel"),
        name="band_attn",
    )(qkv3, qkv3, qkv3, qkv3, qkv3)


def _dsa_proj_kernel(x_ref, w_ref, cos_ref, sin_ref, o_ref, *scratch, dil, rope_tiles):
    j = pl.program_id(1)
    s = x_ref.shape[0]
    ln = s // dil
    acc = jnp.dot(x_ref[...], w_ref[...], preferred_element_type=F32)
    slabs = [slice(c * V7X_LANES, (c + 1) * V7X_LANES) for c in range(acc.shape[1] // V7X_LANES)]
    if dil > 1:
        stage_ref, = scratch
        for c, cols in enumerate(slabs):
            stage_ref[c] = acc[:, cols]

    def emit(rope):
        for r in range(dil):
            rows = slice(r * ln, (r + 1) * ln)
            for c, cols in enumerate(slabs):
                blk = stage_ref[c, pl.ds(r, ln, stride=dil), :] if dil > 1 else acc[:, cols]
                if rope:
                    blk = (blk * cos_ref[rows, :]
                           + pltpu.roll(blk, V7X_LANES // 2, 1) * sin_ref[rows, :])
                o_ref[rows, cols] = blk.astype(o_ref.dtype)

    @pl.when(j < rope_tiles)
    def _():
        emit(True)

    @pl.when(j >= rope_tiles)
    def _():
        emit(False)


def dsa_project(xn, w_g, tables, dil, *, bsz, seq, rope_cols):
    m, d = xn.shape
    n = w_g.shape[1]
    bn = _tile(n, DSA_PROJ_BN)
    assert rope_cols % bn == 0
    residue_major = lambda t: t.reshape(seq // dil, dil, V7X_LANES).transpose(1, 0, 2).reshape(
        seq, V7X_LANES)
    cos, sin = (residue_major(t) for t in tables)
    scratch = [pltpu.VMEM((bn // V7X_LANES, seq, V7X_LANES), F32)] if dil > 1 else []
    return pl.pallas_call(
        functools.partial(_dsa_proj_kernel, dil=dil, rope_tiles=rope_cols // bn),
        out_shape=jax.ShapeDtypeStruct((m, n), BF16),
        grid=(bsz, n // bn),
        in_specs=[pl.BlockSpec((seq, d), lambda b, j: (b, 0)),
                  pl.BlockSpec((d, bn), lambda b, j: (0, j)),
                  pl.BlockSpec((seq, V7X_LANES), lambda b, j: (0, 0)),
                  pl.BlockSpec((seq, V7X_LANES), lambda b, j: (0, 0))],
        out_specs=pl.BlockSpec((seq, bn), lambda b, j: (b, j)),
        scratch_shapes=scratch,
        compiler_params=_params("parallel", "arbitrary"),
        name="dsa_proj",
    )(xn, w_g, cos, sin)


def _merge_kernel(*refs, dils, heads):
    groups = len(dils)
    o_refs = refs[:groups]
    lse_refs = refs[groups:2 * groups]
    out_ref, so_ref, sl_ref = refs[2 * groups:]
    s = out_ref.shape[1]

    def to_sequence_order(dst_ref, g, rows_of):
        ln = s // dils[g]
        for r in range(dils[g]):
            dst_ref[g, pl.ds(r, ln, stride=dils[g]), :] = rows_of(slice(r * ln, (r + 1) * ln))

    for g in range(groups):
        to_sequence_order(sl_ref, g, lambda rows, g=g: lse_refs[g][0, rows, :])
    lses = [sl_ref[g] for g in range(groups)]
    mx = functools.reduce(jnp.maximum, lses)
    es = [jnp.exp(x - mx) for x in lses]
    inv = 1.0 / functools.reduce(lambda a, b: a + b, es)
    alphas = [e * inv for e in es]
    dh = DSA_HEAD_DIM
    for hh in range(heads):
        cols = slice(hh * dh, (hh + 1) * dh)
        acc = None
        for g in range(groups):
            to_sequence_order(so_ref, g,
                              lambda rows, g=g: o_refs[g][0, rows, cols].astype(F32))
            t = alphas[g][:, hh:hh + 1] * so_ref[g]
            acc = t if acc is None else acc + t
        out_ref[0, :, cols] = acc.astype(out_ref.dtype)


def merge_groups(outs, lses, dils, *, heads):
    bsz, seq, hw = outs[0].shape
    groups = len(outs)
    assert DSA_HEAD_DIM == V7X_LANES
    out = pl.pallas_call(
        functools.partial(_merge_kernel, dils=tuple(dils), heads=heads),
        out_shape=jax.ShapeDtypeStruct((bsz, seq, hw), BF16),
        grid=(bsz,),
        in_specs=[pl.BlockSpec((1, seq, hw), lambda b: (b, 0, 0))] * groups
        + [pl.BlockSpec((1, seq, V7X_LANES), lambda b: (b, 0, 0))] * groups,
        out_specs=pl.BlockSpec((1, seq, hw), lambda b: (b, 0, 0)),
        scratch_shapes=[pltpu.VMEM((groups, seq, V7X_LANES), F32),
                        pltpu.VMEM((groups, seq, V7X_LANES), F32)],
        compiler_params=_params("parallel"),
        name="merge_groups",
    )(*outs, *lses)
    return out.reshape(bsz * seq, hw)


def _mla_attn_kernel(q_ref, kv_ref, kpe_ref, o_ref, *, bk):
    qi = pl.program_id(2)
    bq = q_ref.shape[1]
    steps = bq // bk
    rq = lax.broadcasted_iota(jnp.int32, (bq, bk), 0)
    ck = lax.broadcasted_iota(jnp.int32, (bq, bk), 1)

    def scores(q, kb):
        rows = slice(kb * bk, (kb + 1) * bk)
        k = jnp.concatenate([kv_ref[0, rows, :MLA_NOPE],
                             kpe_ref[0, rows, :].astype(BF16)], axis=1)
        return lax.dot_general(q, k, (((1,), (1,)), ((), ())), preferred_element_type=F32)

    def update(m_i, l_i, acc, s, kb):
        v = kv_ref[0, kb * bk:(kb + 1) * bk, MLA_NOPE:]
        m_new = jnp.maximum(m_i, jnp.max(s, axis=-1, keepdims=True))
        alpha = jnp.exp2(m_i - m_new)
        p = jnp.exp2(s - m_new)
        l_new = alpha * l_i + jnp.sum(p, axis=-1, keepdims=True)
        acc = alpha * acc + jnp.dot(p.astype(BF16), v, preferred_element_type=F32)
        return m_new, l_new, acc

    def attend(n_full):
        q = q_ref[0]
        n_blocks = n_full + steps
        m_i = jnp.full((bq, 1), NEG_INF, F32)
        l_i = jnp.zeros((bq, 1), F32)
        acc = jnp.zeros((bq, MLA_V), F32)
        s = scores(q, 0)
        for kb in range(n_blocks):
            s_next = scores(q, kb + 1) if kb + 1 < n_blocks else None
            if kb >= n_full:
                s = jnp.where(ck + (kb - n_full) * bk <= rq, s, NEG_INF)
            m_i, l_i, acc = update(m_i, l_i, acc, s, kb)
            s = s_next
        o_ref[0] = (acc / l_i).astype(o_ref.dtype)

    for blk in range(kv_ref.shape[1] // bq):
        pl.when(qi == blk)(functools.partial(attend, blk * steps))


def mla_attention(q, kv, c, *, bsz, seq, kpe_block):
    hq = MLA_NOPE + V7X_LANES
    heads = q.shape[1] // hq
    bq = _tile(seq, MLA_BQ)
    bk = _tile(bq, MLA_BK)
    q3 = q.reshape(bsz, seq, heads * hq)
    kv3 = kv.reshape(bsz, seq, kv.shape[1])
    c3 = c.reshape(bsz, seq, c.shape[1])
    o = pl.pallas_call(
        functools.partial(_mla_attn_kernel, bk=bk),
        out_shape=jax.ShapeDtypeStruct((bsz, seq, heads * MLA_V), BF16),
        grid=(bsz, heads, seq // bq),
        in_specs=[pl.BlockSpec((1, bq, hq), lambda b, h, i: (b, i, h)),
                  pl.BlockSpec((1, seq, MLA_NOPE + MLA_V), lambda b, h, i: (b, 0, h)),
                  pl.BlockSpec((1, seq, V7X_LANES), lambda b, h, i: (b, 0, kpe_block))],
        out_specs=pl.BlockSpec((1, bq, MLA_V), lambda b, h, i: (b, i, h)),
        compiler_params=_params("parallel", "parallel", "parallel"),
        name="mla_attn",
    )(q3, kv3, c3)
    return o.reshape(bsz * seq, heads * MLA_V)


def _sgu_out_kernel(u_ref, v_ref, st_ref, lng_ref, lnb_ref, ws_ref, bs_ref, wo_ref, bo_ref,
                    h_ref, o_ref, acc_ref, gate_ref, *, width):
    g = pl.program_id(1)
    ng = pl.num_programs(1)
    bm = u_ref.shape[0]
    ch = ws_ref.shape[1]
    st = st_ref[...]
    mu = st[:, 0:1] * (1.0 / width)
    var = st[:, 1:2] * (1.0 / width) - mu * mu
    rstd = lax.rsqrt(var + NORM_EPS)
    ti = lax.broadcasted_iota(jnp.int32, (ch, ch), 0)
    si = lax.broadcasted_iota(jnp.int32, (ch, ch), 1)
    ws = jnp.where(si <= ti, ws_ref[0], 0.0).astype(BF16)
    bs = bs_ref[0]
    lng = lng_ref[...]
    lnb = lnb_ref[...]
    for c in range(bm // ch):
        rows = slice(c * ch, (c + 1) * ch)
        vn = (v_ref[rows, :].astype(F32) - mu[rows]) * rstd[rows] * lng + lnb
        mixed = jnp.dot(ws, vn.astype(BF16), preferred_element_type=F32) + bs
        gate_ref[rows, :] = (u_ref[rows, :].astype(F32) * mixed).astype(BF16)
    part = jnp.dot(gate_ref[...], wo_ref[...], preferred_element_type=F32)

    @pl.when(g == 0)
    def _():
        acc_ref[...] = part

    @pl.when(g > 0)
    def _():
        acc_ref[...] += part

    @pl.when(g == ng - 1)
    def _():
        o_ref[...] = acc_ref[...] + bo_ref[...] + h_ref[...]


def sgu_out(u, v, st, ln_g, ln_b, w_s, b_s, w_out, b_out, h):
    m, e = u.shape
    d = w_out.shape[1]
    groups, ch, _ = w_s.shape
    gw = e // groups
    bm = _tile(m, SGU_BM)
    assert bm % ch == 0
    return pl.pallas_call(
        functools.partial(_sgu_out_kernel, width=e),
        out_shape=jax.ShapeDtypeStruct((m, d), F32),
        grid=(m // bm, groups),
        in_specs=[
            pl.BlockSpec((bm, gw), lambda i, g: (i, g)),
            pl.BlockSpec((bm, gw), lambda i, g: (i, g)),
            pl.BlockSpec((bm, V7X_LANES), lambda i, g: (i, 0)),
            pl.BlockSpec((1, gw), lambda i, g: (0, g)),
            pl.BlockSpec((1, gw), lambda i, g: (0, g)),
            pl.BlockSpec((1, ch, ch), lambda i, g: (g, 0, 0)),
            pl.BlockSpec((1, ch, 1), lambda i, g: (g, 0, 0)),
            pl.BlockSpec((gw, d), lambda i, g: (g, 0)),
            pl.BlockSpec((1, d), lambda i, g: (0, 0)),
            pl.BlockSpec((bm, d), lambda i, g: (i, 0)),
        ],
        out_specs=pl.BlockSpec((bm, d), lambda i, g: (i, 0)),
        scratch_shapes=[pltpu.VMEM((bm, d), F32), pltpu.VMEM((bm, gw), BF16)],
        compiler_params=_params("parallel", "arbitrary"),
        name="sgu_out",
    )(u, v, st, ln_g.reshape(1, e).astype(F32), ln_b.reshape(1, e).astype(F32),
      w_s.astype(F32), b_s.reshape(groups, ch, 1).astype(F32), w_out.astype(BF16),
      b_out.reshape(1, d).astype(F32), h)


def _rope_tables(seq, dim):
    half = dim // 2
    pos = jnp.arange(seq, dtype=F32)
    inv = ROPE_THETA ** (-(jnp.arange(0, dim, 2, dtype=F32) / dim))
    ang = pos[:, None] * inv[None, :]
    zeros = jnp.zeros((seq, V7X_LANES // 2 - half), F32)
    cos = jnp.concatenate([jnp.cos(ang), zeros, jnp.cos(ang), zeros], axis=1)
    sin = jnp.concatenate([-jnp.sin(ang), zeros, jnp.sin(ang), zeros], axis=1)
    return cos, sin


def _spread_rope_cols(w):
    k, dim = w.shape
    half = dim // 2
    zeros = jnp.zeros((k, V7X_LANES // 2 - half), w.dtype)
    return jnp.concatenate([w[:, :half], zeros, w[:, half:], zeros], axis=1)


def conformer_mixer(xn, h, w_in, b_in, dw_w, dw_b, ln_g, ln_b, w_out, b_out, *, bsz, seq):
    c = w_out.shape[0]
    wi = w_in.astype(BF16)
    z = matmul(xn, wi[:, :c], w2=wi[:, c:], bias=b_in[:c], bias2=b_in[c:])
    z = causal_dwconv(z.reshape(bsz, seq, c), dw_w, dw_b).reshape(bsz * seq, c)
    return matmul(z, w_out.astype(BF16), prologue="ln_silu", pg=ln_g, pb=ln_b, bias=b_out,
                  residual=h, out_dtype=F32)


def dilated_attention_mixer(xn, h, w_qkv, w_o, *, bsz, seq):
    d = xn.shape[1]
    dh = DSA_HEAD_DIM
    heads = w_o.shape[0] // dh
    groups = len(DSA_CONFIGS)
    hw = heads * dh
    w5 = w_qkv.reshape(d, groups, 3, hw)
    scale = jnp.array([dh ** -0.5, 1.0, 1.0], F32).reshape(1, 1, 3, 1)
    w5 = (w5 * scale).astype(BF16)
    tables = _rope_tables(seq, dh)
    outs, lses = [], []
    for g, (window, dil) in enumerate(DSA_CONFIGS):
        qkv = dsa_project(xn, w5[:, g].reshape(d, 3 * hw), tables, dil, bsz=bsz, seq=seq,
                          rope_cols=2 * hw)
        o, lse = band_attention(qkv, window, dil, bsz=bsz, seq=seq, heads=heads)
        outs.append(o)
        lses.append(lse)
    o = merge_groups(outs, lses, [dil for _, dil in DSA_CONFIGS], heads=heads)
    return matmul(o, w_o.astype(BF16), residual=h, out_dtype=F32)


def mla_mixer(xn, h, w_in, q_norm, w_qb, kv_norm, w_kvb, w_o, *, bsz, seq):
    q_rank = q_norm.shape[0]
    kv_rank = kv_norm.shape[0]
    assert q_rank == kv_rank
    heads = MLA_HEADS
    w_in_p = jnp.concatenate(
        [w_in[:, :q_rank + kv_rank], _spread_rope_cols(w_in[:, q_rank + kv_rank:])],
        axis=1).astype(BF16)
    n_c = w_in_p.shape[1]
    kpe_group = (q_rank + kv_rank) // V7X_LANES
    cmask = tuple(k == kpe_group for k in range(n_c // V7X_LANES))
    tables = _rope_tables(seq, MLA_ROPE)
    c = matmul(xn, w_in_p, rope=tables, rope_mask=cmask, seq=seq, out_dtype=F32, bn=n_c)
    q_scale = (MLA_NOPE + MLA_ROPE) ** -0.5 * math.log2(math.e)
    wq3 = w_qb.reshape(q_rank, heads, MLA_NOPE + MLA_ROPE) * q_scale
    wq_pe = jax.vmap(_spread_rope_cols, in_axes=1, out_axes=1)(wq3[:, :, MLA_NOPE:])
    wq_p = jnp.concatenate([wq3[:, :, :MLA_NOPE], wq_pe], axis=2)
    wq_p = wq_p.reshape(q_rank, heads * (MLA_NOPE + V7X_LANES)).astype(BF16)
    bn = _tile(wq_p.shape[1], MLA_PROJ_BN)
    qmask = tuple(k % 2 == 1 for k in range(bn // V7X_LANES))
    q = matmul(c, wq_p, x_cols=(0, q_rank), prologue="rms", pg=q_norm, rope=tables,
               rope_mask=qmask, seq=seq, bm=MM_BM, bn=bn)
    kv = matmul(c, w_kvb.astype(BF16), x_cols=(1, kv_rank), prologue="rms", pg=kv_norm,
                bm=MM_BM, bn=MLA_PROJ_BN)
    o = mla_attention(q, kv, c, bsz=bsz, seq=seq, kpe_block=kpe_group)
    return matmul(o, w_o.astype(BF16), residual=h, out_dtype=F32)


def sgu_mixer(xn, h, w_in, b_in, ln_g, ln_b, w_s, b_s, w_out, b_out):
    e = w_out.shape[0]
    wi = w_in.astype(BF16)
    u = matmul(xn, wi[:, :e], bias=b_in[:e], gelu=True)
    v, st = matmul(xn, wi[:, e:], bias=b_in[e:], gelu=True, stats=True)
    return sgu_out(u, v, st, ln_g, ln_b, w_s, b_s, w_out, b_out, h)


def kernel(x, l0_norm_mix, l0_cc_w_in, l0_cc_b_in, l0_cc_dw_w, l0_cc_dw_b, l0_cc_ln_g, l0_cc_ln_b, l0_cc_w_out, l0_cc_b_out, l0_norm_ffn, l0_ffn_w_up, l0_ffn_dw_w, l0_ffn_dw_b, l0_ffn_w_down, l1_norm_mix, l1_dsa_w_qkv, l1_dsa_w_o, l1_norm_ffn, l1_ffn_w_up, l1_ffn_dw_w, l1_ffn_dw_b, l1_ffn_w_down, l2_norm_mix, l2_mla_w_in, l2_mla_q_norm, l2_mla_w_qb, l2_mla_kv_norm, l2_mla_w_kvb, l2_mla_w_o, l2_norm_ffn, l2_ffn_w_up, l2_ffn_dw_w, l2_ffn_dw_b, l2_ffn_w_down, l3_norm_mix, l3_sg_w_in, l3_sg_b_in, l3_sg_ln_g, l3_sg_ln_b, l3_sg_w_s, l3_sg_b_s, l3_sg_w_out, l3_sg_b_out, l3_norm_ffn, l3_ffn_w_up, l3_ffn_dw_w, l3_ffn_dw_b, l3_ffn_w_down, final_norm):
    bsz, seq, d = x.shape
    dims = dict(bsz=bsz, seq=seq)
    h = x.reshape(bsz * seq, d)

    def ffn(h, norm, w_up, dw_w, dw_b, w_down, **tail):
        return conv_ffn(h, norm, w_up, dw_w, dw_b, w_down, seq=seq, **tail)

    feeds_next = dict(tail="residual_and_norm")
    h = conformer_mixer(rmsnorm(h, l0_norm_mix, BF16), h, l0_cc_w_in, l0_cc_b_in, l0_cc_dw_w,
                        l0_cc_dw_b, l0_cc_ln_g, l0_cc_ln_b, l0_cc_w_out, l0_cc_b_out, **dims)
    h, xn = ffn(h, l0_norm_ffn, l0_ffn_w_up, l0_ffn_dw_w, l0_ffn_dw_b, l0_ffn_w_down,
                tail_g=l1_norm_mix, **feeds_next)
    h = dilated_attention_mixer(xn, h, l1_dsa_w_qkv, l1_dsa_w_o, **dims)
    h, xn = ffn(h, l1_norm_ffn, l1_ffn_w_up, l1_ffn_dw_w, l1_ffn_dw_b, l1_ffn_w_down,
                tail_g=l2_norm_mix, **feeds_next)
    h = mla_mixer(xn, h, l2_mla_w_in, l2_mla_q_norm, l2_mla_w_qb, l2_mla_kv_norm, l2_mla_w_kvb,
                  l2_mla_w_o, **dims)
    h, xn = ffn(h, l2_norm_ffn, l2_ffn_w_up, l2_ffn_dw_w, l2_ffn_dw_b, l2_ffn_w_down,
                tail_g=l3_norm_mix, **feeds_next)
    h = sgu_mixer(xn, h, l3_sg_w_in, l3_sg_b_in, l3_sg_ln_g, l3_sg_ln_b, l3_sg_w_s, l3_sg_b_s,
                  l3_sg_w_out, l3_sg_b_out)
    out = ffn(h, l3_norm_ffn, l3_ffn_w_up, l3_ffn_dw_w, l3_ffn_dw_b, l3_ffn_w_down,
              tail="norm_only", tail_g=final_norm)
    return out.reshape(bsz, seq, d)
```

```python
import functools
import math

import jax
import jax.numpy as jnp
from jax import lax
from jax.experimental import pallas as pl
from jax.experimental.pallas import tpu as pltpu

F32 = jnp.float32
BF16 = jnp.bfloat16

NORM_EPS = 1e-6
ROPE_THETA = 10000.0
NEG_INF = -1e30

DSA_HEAD_DIM = 128
DSA_CONFIGS = ((128, 1), (512, 4), (2048, 16))
DSA_BLOCK = 128
MLA_HEADS = 16
MLA_NOPE = 128
MLA_ROPE = 64
MLA_V = 128
SG_CHUNK = 128
SG_GROUPS = 8

V7X_LANES = 128
V7X_SUBLANES = 8
V7X_VMEM_LIMIT_BYTES = 56 * 1024 * 1024

MM_BM, MM_BN = 1024, 1024
MM_PROLOGUE_BM = 1024
MM_PROLOGUE_ROWS = 128
FFN_BM, FFN_FC = 512, 512
FFN_SUB, FFN_RSUB = 512, 256
MLA_BQ, MLA_BK = 512, 512
SGU_BM = 512
CONV_CB, CONV_ROWS = 256, 64
DSA_PROJ_BN = 512
MLA_PROJ_BN = 2048


def _params(*semantics):
    return pltpu.CompilerParams(dimension_semantics=semantics,
                                vmem_limit_bytes=V7X_VMEM_LIMIT_BYTES)


def _sigmoid(x):
    return 1.0 / (1.0 + jnp.exp(-x))


def _tile(n, want):
    t = min(n, want)
    while n % t:
        t -= 1
    return t


def _rmsnorm_kernel(x_ref, g_ref, o_ref):
    x = x_ref[...].astype(F32)
    y = x * lax.rsqrt(jnp.mean(x * x, axis=-1, keepdims=True) + NORM_EPS)
    o_ref[...] = (y * g_ref[...]).astype(o_ref.dtype)


def rmsnorm(x, g, out_dtype):
    m, d = x.shape
    bm = _tile(m, 512)
    return pl.pallas_call(
        _rmsnorm_kernel,
        out_shape=jax.ShapeDtypeStruct((m, d), out_dtype),
        grid=(m // bm,),
        in_specs=[pl.BlockSpec((bm, d), lambda i: (i, 0)),
                  pl.BlockSpec((1, d), lambda i: (0, 0))],
        out_specs=pl.BlockSpec((bm, d), lambda i: (i, 0)),
        compiler_params=_params("parallel"),
        name="rmsnorm",
    )(x, g.reshape(1, d).astype(F32))


def _mm_kernel(*refs, prologue, has_bias, glu, gelu, rope_mask, residual, stats):
    it = iter(refs)
    x_ref = next(it)
    pg_ref = next(it) if prologue else None
    pb_ref = next(it) if prologue == "ln_silu" else None
    w_ref = next(it)
    w2_ref = next(it) if glu else None
    b_ref = next(it) if has_bias else None
    b2_ref = next(it) if (glu and has_bias) else None
    cos_ref = next(it) if rope_mask else None
    sin_ref = next(it) if rope_mask else None
    res_ref = next(it) if residual else None
    o_ref = next(it)
    st_ref = next(it) if stats else None
    xs_ref = next(it) if prologue else None

    j = pl.program_id(1)

    if prologue:
        @pl.when(j == 0)
        def _():
            chunk = _tile(x_ref.shape[0], MM_PROLOGUE_ROWS)
            for r0 in range(0, x_ref.shape[0], chunk):
                xf = x_ref[r0:r0 + chunk, :].astype(F32)
                if prologue == "rms":
                    y = xf * lax.rsqrt(jnp.mean(xf * xf, axis=-1, keepdims=True) + NORM_EPS)
                    y = y * pg_ref[...]
                else:
                    mu = jnp.mean(xf, axis=-1, keepdims=True)
                    xc = xf - mu
                    y = xc * lax.rsqrt(jnp.mean(xc * xc, axis=-1, keepdims=True) + NORM_EPS)
                    y = y * pg_ref[...] + pb_ref[...]
                    y = y * _sigmoid(y)
                xs_ref[r0:r0 + chunk, :] = y.astype(BF16)
        xv = xs_ref[...]
    else:
        xv = x_ref[...]

    acc = jnp.dot(xv, w_ref[...], preferred_element_type=F32)
    if has_bias:
        acc = acc + b_ref[...]
    if glu:
        gate = jnp.dot(xv, w2_ref[...], preferred_element_type=F32)
        if has_bias:
            gate = gate + b2_ref[...]
        acc = acc * _sigmoid(gate)
    if gelu:
        acc = 0.5 * acc * (1.0 + lax.erf(acc * (2.0 ** -0.5)))
    if stats:
        s1 = jnp.sum(acc, axis=-1, keepdims=True)
        s2 = jnp.sum(acc * acc, axis=-1, keepdims=True)
        lane = lax.broadcasted_iota(jnp.int32, st_ref.shape, 1)
        upd = jnp.where(lane == 0, s1, jnp.where(lane == 1, s2, 0.0))

        @pl.when(j == 0)
        def _():
            st_ref[...] = upd

        @pl.when(j > 0)
        def _():
            st_ref[...] += upd
    if residual:
        acc = acc + res_ref[...]
    if rope_mask:
        cos = cos_ref[...]
        sin = sin_ref[...]
        for k, on in enumerate(rope_mask):
            sl = acc[:, k * V7X_LANES:(k + 1) * V7X_LANES]
            if on:
                sl = sl * cos + pltpu.roll(sl, V7X_LANES // 2, 1) * sin
            o_ref[:, k * V7X_LANES:(k + 1) * V7X_LANES] = sl.astype(o_ref.dtype)
    else:
        o_ref[...] = acc.astype(o_ref.dtype)


def matmul(x, w, *, x_cols=None, prologue=None, pg=None, pb=None, w2=None, bias=None,
           bias2=None, gelu=False, rope=None, rope_mask=None, seq=None, residual=None,
           stats=False, out_dtype=BF16, bm=None, bn=None):
    m = x.shape[0]
    k, n = w.shape
    kblk = 0 if x_cols is None else x_cols[0]
    bm = _tile(m, bm or (MM_PROLOGUE_BM if prologue else MM_BM))
    bn = _tile(n, bn or MM_BN)
    glu = w2 is not None
    has_bias = bias is not None
    if rope_mask:
        assert bn % V7X_LANES == 0 and len(rope_mask) == bn // V7X_LANES
        assert seq % bm == 0
    args, specs = [x], [pl.BlockSpec((bm, k), lambda i, j: (i, kblk))]
    if prologue:
        args.append(pg.reshape(1, k).astype(F32))
        specs.append(pl.BlockSpec((1, k), lambda i, j: (0, 0)))
        if prologue == "ln_silu":
            args.append(pb.reshape(1, k).astype(F32))
            specs.append(pl.BlockSpec((1, k), lambda i, j: (0, 0)))
    args.append(w)
    specs.append(pl.BlockSpec((k, bn), lambda i, j: (0, j)))
    if glu:
        args.append(w2)
        specs.append(pl.BlockSpec((k, bn), lambda i, j: (0, j)))
    if has_bias:
        args.append(bias.reshape(1, n).astype(F32))
        specs.append(pl.BlockSpec((1, bn), lambda i, j: (0, j)))
        if glu:
            args.append(bias2.reshape(1, n).astype(F32))
            specs.append(pl.BlockSpec((1, bn), lambda i, j: (0, j)))
    if rope_mask:
        tiles_per_seq = seq // bm
        for t in rope:
            args.append(t)
            specs.append(pl.BlockSpec((bm, V7X_LANES), lambda i, j: (i % tiles_per_seq, 0)))
    if residual is not None:
        args.append(residual)
        specs.append(pl.BlockSpec((bm, bn), lambda i, j: (i, j)))
    out_shape = [jax.ShapeDtypeStruct((m, n), out_dtype)]
    out_specs = [pl.BlockSpec((bm, bn), lambda i, j: (i, j))]
    if stats:
        out_shape.append(jax.ShapeDtypeStruct((m, V7X_LANES), F32))
        out_specs.append(pl.BlockSpec((bm, V7X_LANES), lambda i, j: (i, 0)))
    scratch = [pltpu.VMEM((bm, k), BF16)] if prologue else []
    kern = functools.partial(
        _mm_kernel, prologue=prologue, has_bias=has_bias, glu=glu, gelu=gelu,
        rope_mask=tuple(rope_mask) if rope_mask else None,
        residual=residual is not None, stats=stats)
    out = pl.pallas_call(
        kern,
        out_shape=out_shape,
        grid=(m // bm, n // bn),
        in_specs=specs,
        out_specs=out_specs,
        scratch_shapes=scratch,
        compiler_params=_params("parallel", "arbitrary"),
        name="matmul",
    )(*args)
    return out if stats else out[0]


def _dwconv_kernel(x_ref, w_ref, b_ref, o_ref, sh_ref, *, width, row_chunk):
    s, cb = x_ref.shape[1], x_ref.shape[2]
    pad = sh_ref.shape[1] - s
    x = x_ref[0].astype(F32)
    xz = jnp.concatenate([x, jnp.zeros((V7X_SUBLANES, cb), F32)], axis=0)
    zero_top = jnp.zeros((pad, cb), F32)
    for r in range(V7X_SUBLANES):
        sh_ref[r, :pad, :] = zero_top
        shifted = xz if r == 0 else pltpu.roll(xz, r, 0)
        sh_ref[r, pad:, :] = shifted[:s]

    w = w_ref[...]
    bias = b_ref[...]

    def chunk(c, carry):
        base = pl.multiple_of(c * row_chunk, row_chunk)
        acc = jnp.zeros((row_chunk, cb), F32) + bias
        for kk in range(width):
            shift = width - 1 - kk
            q, r = divmod(shift, V7X_SUBLANES)
            start = pl.multiple_of(base + (pad - q * V7X_SUBLANES), V7X_SUBLANES)
            tap = sh_ref[r, pl.ds(start, row_chunk), :]
            acc = acc + tap * w[kk:kk + 1, :]
        o_ref[0, pl.ds(base, row_chunk), :] = acc.astype(o_ref.dtype)
        return carry

    lax.fori_loop(0, s // row_chunk, chunk, 0)


def causal_dwconv(x, w, b):
    bsz, s, c = x.shape
    width = w.shape[0]
    cb = _tile(c, CONV_CB)
    row_chunk = _tile(s, CONV_ROWS)
    pad = -(-(width - 1) // V7X_SUBLANES) * V7X_SUBLANES
    return pl.pallas_call(
        functools.partial(_dwconv_kernel, width=width, row_chunk=row_chunk),
        out_shape=jax.ShapeDtypeStruct((bsz, s, c), BF16),
        grid=(bsz, c // cb),
        in_specs=[pl.BlockSpec((1, s, cb), lambda i, j: (i, 0, j)),
                  pl.BlockSpec((width, cb), lambda i, j: (0, j)),
                  pl.BlockSpec((1, cb), lambda i, j: (0, j))],
        out_specs=pl.BlockSpec((1, s, cb), lambda i, j: (i, 0, j)),
        scratch_shapes=[pltpu.VMEM((V7X_SUBLANES, pad + s, cb), F32)],
        compiler_params=_params("parallel", "parallel"),
        name="dwconv",
    )(x, w.astype(F32), b.reshape(1, c).astype(F32))


def _ffn_kernel(h_ref, ng_ref, og_ref, wg_ref, wa_ref, cwg_ref, cwa_ref, cbg_ref, cba_ref,
                wd_ref, o_ref, *rest, tiles_per_seq, nf, sub, rsub, tail):
    xn_ref = rest[0] if tail == "residual_and_norm" else None
    xs_ref, acc_ref, carry_g_ref, carry_a_ref = rest[-4:]
    i = pl.program_id(0)
    j = pl.program_id(1)
    bm, d = h_ref.shape
    fc = wg_ref.shape[1]
    seq_start = (i % tiles_per_seq) == 0
    sl = V7X_SUBLANES
    nv = rsub // sl
    slabs = [slice(c * V7X_LANES, (c + 1) * V7X_LANES) for c in range(d // V7X_LANES)]

    na = nv // sl

    def staged(r, a, k):
        return pl.ds(r * rsub + sl * sl * a + k, sl, stride=sl)

    @pl.when(seq_start)
    def _():
        carry_g_ref[j] = jnp.zeros((2 * sl, fc), F32)
        carry_a_ref[j] = jnp.zeros((2 * sl, fc), F32)

    def normalise(r):
        gain = ng_ref[...]
        for s in range(sl):
            for a in range(na):
                src = r * rsub + nv * s + sl * a
                dst = r * rsub + sl * sl * a + sl * s
                hv = h_ref[src:src + sl, :]
                y = hv * lax.rsqrt(jnp.mean(hv * hv, axis=-1, keepdims=True) + NORM_EPS) * gain
                for c, cols in enumerate(slabs):
                    acc_ref[c, dst:dst + sl, :] = y[:, cols]
        for a in range(na):
            for b in range(0, sl, 2):
                dst = r * rsub + sl * (sl * a + b)
                for c, cols in enumerate(slabs):
                    pair = jnp.concatenate([acc_ref[c, staged(r, a, b), :],
                                            acc_ref[c, staged(r, a, b + 1), :]], axis=0)
                    xs_ref[dst:dst + 2 * sl, cols] = pair.astype(BF16)
        for c in range(len(slabs)):
            acc_ref[c, r * rsub:(r + 1) * rsub, :] = jnp.zeros((rsub, V7X_LANES), F32)

    def finish(r):
        for s in range(sl):
            for a in range(0, na, 2):
                dst = r * rsub + nv * s + sl * a
                rows = slice(dst, dst + 2 * sl)
                vals = [jnp.concatenate([acc_ref[c, staged(r, a, s), :],
                                         acc_ref[c, staged(r, a + 1, s), :]], axis=0)
                        + h_ref[rows, cols] for c, cols in enumerate(slabs)]
                if tail != "residual":
                    ss = functools.reduce(
                        lambda p, q: p + q,
                        [jnp.sum(v * v, axis=-1, keepdims=True) for v in vals])
                    inv = lax.rsqrt(ss * (1.0 / d) + NORM_EPS)
                    normed = [v * inv * og_ref[:, cols] for v, cols in zip(vals, slabs)]
                if tail == "norm_only":
                    vals = normed
                for v, cols in zip(vals, slabs):
                    o_ref[rows, cols] = v
                if tail == "residual_and_norm":
                    for v, cols in zip(normed, slabs):
                        xn_ref[rows, cols] = v.astype(BF16)

    first_row = lax.broadcasted_iota(jnp.int32, (sl, sub), 0) == 0

    def shift1(z, prev_row):
        top = jnp.where(first_row, prev_row, pltpu.roll(z[rsub - sl:, :], 1, 0))
        return jnp.concatenate([top, z[:rsub - sl, :]], axis=0)

    def conv(z, prev, cw, cb):
        z1 = shift1(z, prev[2 * sl - 1:2 * sl, :])
        z2 = shift1(z1, prev[sl - 1:sl, :])
        return z2 * cw[0:1, :] + z1 * cw[1:2, :] + z * cw[2:3, :] + cb

    n_r = bm // rsub
    tiles = [(r, slice(c * sub, (c + 1) * sub)) for c in range(fc // sub) for r in range(n_r)]

    def up(r, cols):
        xr = xs_ref[r * rsub:(r + 1) * rsub, :]
        return (jnp.dot(xr, wg_ref[:, cols], preferred_element_type=F32),
                jnp.dot(xr, wa_ref[:, cols], preferred_element_type=F32))

    def step(first, last):
        if first:
            for r in range(n_r):
                normalise(r)
        z_next = up(*tiles[0])
        for k, (r, cols) in enumerate(tiles):
            rows = slice(r * rsub, (r + 1) * rsub)
            zg, za = z_next
            if k + 1 < len(tiles):
                z_next = up(*tiles[k + 1])
            if r == 0:
                prev_g = carry_g_ref[j, :, cols]
                prev_a = carry_a_ref[j, :, cols]
            g = conv(zg, prev_g, cwg_ref[:, cols], cbg_ref[:, cols])
            a = conv(za, prev_a, cwa_ref[:, cols], cba_ref[:, cols])
            prev_g = zg[rsub - 2 * sl:, :]
            prev_a = za[rsub - 2 * sl:, :]
            if r == n_r - 1:
                carry_g_ref[j, :, cols] = prev_g
                carry_a_ref[j, :, cols] = prev_a
            hg = 0.5 * g
            act = ((hg + hg * jnp.tanh(hg)) * a).astype(BF16)
            part = jnp.dot(act, wd_ref[cols, :], preferred_element_type=F32)
            for c, dcols in enumerate(slabs):
                acc_ref[c, rows, :] += part[:, dcols]
            if last and k >= len(tiles) - n_r:
                finish(r)

    if nf == 1:
        step(True, True)
    else:
        pl.when(j == 0)(functools.partial(step, True, False))
        pl.when((j > 0) & (j < nf - 1))(functools.partial(step, False, False))
        pl.when(j == nf - 1)(functools.partial(step, False, True))


def conv_ffn(h, norm_g, w_up, dw_w, dw_b, w_down, *, seq, tail="residual", tail_g=None):
    m, d = h.shape
    f = w_down.shape[0]
    bm = _tile(seq, FFN_BM)
    fc = _tile(f, FFN_FC)
    nf = f // fc
    wu = w_up.astype(BF16)
    wd = w_down.astype(BF16)
    cw = dw_w.astype(F32)
    cb = dw_b.reshape(1, 2 * f).astype(F32)
    rsub = _tile(bm, FFN_RSUB)
    assert rsub % (2 * V7X_SUBLANES * V7X_SUBLANES) == 0 and d % V7X_LANES == 0
    kern = functools.partial(_ffn_kernel, tiles_per_seq=seq // bm, nf=nf,
                             sub=_tile(fc, FFN_SUB), rsub=rsub, tail=tail)
    gains = [g.reshape(1, d).astype(F32) for g in (norm_g, norm_g if tail_g is None else tail_g)]
    row_tile = pl.BlockSpec((bm, d), lambda i, j: (i, 0))
    out_shape = [jax.ShapeDtypeStruct((m, d), F32)]
    if tail == "residual_and_norm":
        out_shape.append(jax.ShapeDtypeStruct((m, d), BF16))
    out = pl.pallas_call(
        kern,
        out_shape=out_shape,
        grid=(m // bm, nf),
        in_specs=[
            row_tile,
            pl.BlockSpec((1, d), lambda i, j: (0, 0)),
            pl.BlockSpec((1, d), lambda i, j: (0, 0)),
            pl.BlockSpec((d, fc), lambda i, j: (0, j)),
            pl.BlockSpec((d, fc), lambda i, j: (0, j + nf)),
            pl.BlockSpec((3, fc), lambda i, j: (0, j)),
            pl.BlockSpec((3, fc), lambda i, j: (0, j + nf)),
            pl.BlockSpec((1, fc), lambda i, j: (0, j)),
            pl.BlockSpec((1, fc), lambda i, j: (0, j + nf)),
            pl.BlockSpec((fc, d), lambda i, j: (j, 0)),
        ],
        out_specs=[row_tile] * len(out_shape),
        scratch_shapes=[
            pltpu.VMEM((bm, d), BF16),
            pltpu.VMEM((d // V7X_LANES, bm, V7X_LANES), F32),
            pltpu.VMEM((nf, 2 * V7X_SUBLANES, fc), F32),
            pltpu.VMEM((nf, 2 * V7X_SUBLANES, fc), F32),
        ],
        compiler_params=_params("arbitrary", "arbitrary"),
        name="conv_ffn",
    )(h, *gains, wu, wu, cw, cw, cb, cb, wd)
    return out if tail == "residual_and_norm" else out[0]


def _band_attn_kernel(q_ref, kp_ref, kc_ref, vp_ref, vc_ref, o_ref, lse_ref, *, heads, span):
    n = pl.program_id(2)
    blk = q_ref.shape[1]
    qi = lax.broadcasted_iota(jnp.int32, (blk, 2 * blk), 0)
    kj = lax.broadcasted_iota(jnp.int32, (blk, 2 * blk), 1)
    dist = blk + qi - kj
    first_key = jnp.where(n > 0, 0, blk)
    mask = (dist >= 0) & (dist <= span) & (kj >= first_key)
    lane = lax.broadcasted_iota(jnp.int32, (blk, V7X_LANES), 1)
    lse_tile = jnp.zeros((blk, V7X_LANES), F32)
    dh = DSA_HEAD_DIM

    def scores(hh):
        cols = slice(hh * dh, (hh + 1) * dh)
        k = jnp.concatenate([kp_ref[0, :, cols], kc_ref[0, :, cols]], axis=0)
        s = lax.dot_general(q_ref[0, :, cols], k, (((1,), (1,)), ((), ())),
                            preferred_element_type=F32)
        return jnp.where(mask, s, NEG_INF)

    s_next = scores(0)
    for hh in range(heads):
        cols = slice(hh * dh, (hh + 1) * dh)
        s = s_next
        if hh + 1 < heads:
            s_next = scores(hh + 1)
        v = jnp.concatenate([vp_ref[0, :, cols], vc_ref[0, :, cols]], axis=0)
        mx = jnp.max(s, axis=-1, keepdims=True)
        p = jnp.exp(s - mx)
        l = jnp.sum(p, axis=-1, keepdims=True)
        o = jnp.dot(p.astype(BF16), v, preferred_element_type=F32) / l
        o_ref[0, :, cols] = o.astype(o_ref.dtype)
        lse_tile = jnp.where(lane == hh, mx + jnp.log(l), lse_tile)
    lse_ref[0] = lse_tile


def band_attention(qkv, window, dil, *, bsz, seq, heads):
    hw = heads * DSA_HEAD_DIM
    blk = DSA_BLOCK
    nb = seq // dil // blk
    span = window // dil
    qkv3 = qkv.reshape(bsz, seq, 3 * hw)
    cur = lambda r, n: r * nb + n
    prev = lambda r, n: r * nb + jnp.maximum(n - 1, 0)
    return pl.pallas_call(
        functools.partial(_band_attn_kernel, heads=heads, span=span),
        out_shape=[jax.ShapeDtypeStruct((bsz, seq, hw), BF16),
                   jax.ShapeDtypeStruct((bsz, seq, V7X_LANES), F32)],
        grid=(bsz, dil, nb),
        in_specs=[
            pl.BlockSpec((1, blk, hw), lambda b, r, n: (b, cur(r, n), 0)),
            pl.BlockSpec((1, blk, hw), lambda b, r, n: (b, prev(r, n), 1)),
            pl.BlockSpec((1, blk, hw), lambda b, r, n: (b, cur(r, n), 1)),
            pl.BlockSpec((1, blk, hw), lambda b, r, n: (b, prev(r, n), 2)),
            pl.BlockSpec((1, blk, hw), lambda b, r, n: (b, cur(r, n), 2)),
        ],
        out_specs=[pl.BlockSpec((1, blk, hw), lambda b, r, n: (b, cur(r, n), 0)),
                   pl.BlockSpec((1, blk, V7X_LANES), lambda b, r, n: (b, cur(r, n), 0))],
        compiler_params=_params("parallel", "parallel", "parallel"),
        name="band_attn",
    )(qkv3, qkv3, qkv3, qkv3, qkv3)


def _dsa_proj_kernel(x_ref, w_ref, cos_ref, sin_ref, o_ref, *scratch, dil, rope_tiles):
    j = pl.program_id(1)
    s = x_ref.shape[0]
    ln = s // dil
    acc = jnp.dot(x_ref[...], w_ref[...], preferred_element_type=F32)
    slabs = [slice(c * V7X_LANES, (c + 1) * V7X_LANES) for c in range(acc.shape[1] // V7X_LANES)]
    if dil > 1:
        stage_ref, = scratch
        for c, cols in enumerate(slabs):
            stage_ref[c] = acc[:, cols]

    def emit(rope):
        for r in range(dil):
            rows = slice(r * ln, (r + 1) * ln)
            for c, cols in enumerate(slabs):
                blk = stage_ref[c, pl.ds(r, ln, stride=dil), :] if dil > 1 else acc[:, cols]
                if rope:
                    blk = (blk * cos_ref[rows, :]
                           + pltpu.roll(blk, V7X_LANES // 2, 1) * sin_ref[rows, :])
                o_ref[rows, cols] = blk.astype(o_ref.dtype)

    @pl.when(j < rope_tiles)
    def _():
        emit(True)

    @pl.when(j >= rope_tiles)
    def _():
        emit(False)


def dsa_project(xn, w_g, tables, dil, *, bsz, seq, rope_cols):
    m, d = xn.shape
    n = w_g.shape[1]
    bn = _tile(n, DSA_PROJ_BN)
    assert rope_cols % bn == 0
    residue_major = lambda t: t.reshape(seq // dil, dil, V7X_LANES).transpose(1, 0, 2).reshape(
        seq, V7X_LANES)
    cos, sin = (residue_major(t) for t in tables)
    scratch = [pltpu.VMEM((bn // V7X_LANES, seq, V7X_LANES), F32)] if dil > 1 else []
    return pl.pallas_call(
        functools.partial(_dsa_proj_kernel, dil=dil, rope_tiles=rope_cols // bn),
        out_shape=jax.ShapeDtypeStruct((m, n), BF16),
        grid=(bsz, n // bn),
        in_specs=[pl.BlockSpec((seq, d), lambda b, j: (b, 0)),
                  pl.BlockSpec((d, bn), lambda b, j: (0, j)),
                  pl.BlockSpec((seq, V7X_LANES), lambda b, j: (0, 0)),
                  pl.BlockSpec((seq, V7X_LANES), lambda b, j: (0, 0))],
        out_specs=pl.BlockSpec((seq, bn), lambda b, j: (b, j)),
        scratch_shapes=scratch,
        compiler_params=_params("parallel", "arbitrary"),
        name="dsa_proj",
    )(xn, w_g, cos, sin)


def _merge_kernel(*refs, dils, heads):
    groups = len(dils)
    o_refs = refs[:groups]
    lse_refs = refs[groups:2 * groups]
    out_ref, so_ref, sl_ref = refs[2 * groups:]
    s = out_ref.shape[1]

    def to_sequence_order(dst_ref, g, rows_of):
        ln = s // dils[g]
        for r in range(dils[g]):
            dst_ref[g, pl.ds(r, ln, stride=dils[g]), :] = rows_of(slice(r * ln, (r + 1) * ln))

    for g in range(groups):
        to_sequence_order(sl_ref, g, lambda rows, g=g: lse_refs[g][0, rows, :])
    lses = [sl_ref[g] for g in range(groups)]
    mx = functools.reduce(jnp.maximum, lses)
    es = [jnp.exp(x - mx) for x in lses]
    inv = 1.0 / functools.reduce(lambda a, b: a + b, es)
    alphas = [e * inv for e in es]
    dh = DSA_HEAD_DIM
    for hh in range(heads):
        cols = slice(hh * dh, (hh + 1) * dh)
        acc = None
        for g in range(groups):
            to_sequence_order(so_ref, g,
                              lambda rows, g=g: o_refs[g][0, rows, cols].astype(F32))
            t = alphas[g][:, hh:hh + 1] * so_ref[g]
            acc = t if acc is None else acc + t
        out_ref[0, :, cols] = acc.astype(out_ref.dtype)


def merge_groups(outs, lses, dils, *, heads):
    bsz, seq, hw = outs[0].shape
    groups = len(outs)
    assert DSA_HEAD_DIM == V7X_LANES
    out = pl.pallas_call(
        functools.partial(_merge_kernel, dils=tuple(dils), heads=heads),
        out_shape=jax.ShapeDtypeStruct((bsz, seq, hw), BF16),
        grid=(bsz,),
        in_specs=[pl.BlockSpec((1, seq, hw), lambda b: (b, 0, 0))] * groups
        + [pl.BlockSpec((1, seq, V7X_LANES), lambda b: (b, 0, 0))] * groups,
        out_specs=pl.BlockSpec((1, seq, hw), lambda b: (b, 0, 0)),
        scratch_shapes=[pltpu.VMEM((groups, seq, V7X_LANES), F32),
                        pltpu.VMEM((groups, seq, V7X_LANES), F32)],
        compiler_params=_params("parallel"),
        name="merge_groups",
    )(*outs, *lses)
    return out.reshape(bsz * seq, hw)


def _mla_attn_kernel(q_ref, kv_ref, kpe_ref, o_ref, *, bk):
    qi = pl.program_id(2)
    bq = q_ref.shape[1]
    steps = bq // bk
    rq = lax.broadcasted_iota(jnp.int32, (bq, bk), 0)
    ck = lax.broadcasted_iota(jnp.int32, (bq, bk), 1)

    def scores(q, kb):
        rows = slice(kb * bk, (kb + 1) * bk)
        k = jnp.concatenate([kv_ref[0, rows, :MLA_NOPE],
                             kpe_ref[0, rows, :].astype(BF16)], axis=1)
        return lax.dot_general(q, k, (((1,), (1,)), ((), ())), preferred_element_type=F32)

    def update(m_i, l_i, acc, s, kb):
        v = kv_ref[0, kb * bk:(kb + 1) * bk, MLA_NOPE:]
        m_new = jnp.maximum(m_i, jnp.max(s, axis=-1, keepdims=True))
        alpha = jnp.exp2(m_i - m_new)
        p = jnp.exp2(s - m_new)
        l_new = alpha * l_i + jnp.sum(p, axis=-1, keepdims=True)
        acc = alpha * acc + jnp.dot(p.astype(BF16), v, preferred_element_type=F32)
        return m_new, l_new, acc

    def attend(n_full):
        q = q_ref[0]
        n_blocks = n_full + steps
        m_i = jnp.full((bq, 1), NEG_INF, F32)
        l_i = jnp.zeros((bq, 1), F32)
        acc = jnp.zeros((bq, MLA_V), F32)
        s = scores(q, 0)
        for kb in range(n_blocks):
            s_next = scores(q, kb + 1) if kb + 1 < n_blocks else None
            if kb >= n_full:
                s = jnp.where(ck + (kb - n_full) * bk <= rq, s, NEG_INF)
            m_i, l_i, acc = update(m_i, l_i, acc, s, kb)
            s = s_next
        o_ref[0] = (acc / l_i).astype(o_ref.dtype)

    for blk in range(kv_ref.shape[1] // bq):
        pl.when(qi == blk)(functools.partial(attend, blk * steps))


def mla_attention(q, kv, c, *, bsz, seq, kpe_block):
    hq = MLA_NOPE + V7X_LANES
    heads = q.shape[1] // hq
    bq = _tile(seq, MLA_BQ)
    bk = _tile(bq, MLA_BK)
    q3 = q.reshape(bsz, seq, heads * hq)
    kv3 = kv.reshape(bsz, seq, kv.shape[1])
    c3 = c.reshape(bsz, seq, c.shape[1])
    o = pl.pallas_call(
        functools.partial(_mla_attn_kernel, bk=bk),
        out_shape=jax.ShapeDtypeStruct((bsz, seq, heads * MLA_V), BF16),
        grid=(bsz, heads, seq // bq),
        in_specs=[pl.BlockSpec((1, bq, hq), lambda b, h, i: (b, i, h)),
                  pl.BlockSpec((1, seq, MLA_NOPE + MLA_V), lambda b, h, i: (b, 0, h)),
                  pl.BlockSpec((1, seq, V7X_LANES), lambda b, h, i: (b, 0, kpe_block))],
        out_specs=pl.BlockSpec((1, bq, MLA_V), lambda b, h, i: (b, i, h)),
        compiler_params=_params("parallel", "parallel", "parallel"),
        name="mla_attn",
    )(q3, kv3, c3)
    return o.reshape(bsz * seq, heads * MLA_V)


def _sgu_out_kernel(u_ref, v_ref, st_ref, lng_ref, lnb_ref, ws_ref, bs_ref, wo_ref, bo_ref,
                    h_ref, o_ref, acc_ref, gate_ref, *, width):
    g = pl.program_id(1)
    ng = pl.num_programs(1)
    bm = u_ref.shape[0]
    ch = ws_ref.shape[1]
    st = st_ref[...]
    mu = st[:, 0:1] * (1.0 / width)
    var = st[:, 1:2] * (1.0 / width) - mu * mu
    rstd = lax.rsqrt(var + NORM_EPS)
    ti = lax.broadcasted_iota(jnp.int32, (ch, ch), 0)
    si = lax.broadcasted_iota(jnp.int32, (ch, ch), 1)
    ws = jnp.where(si <= ti, ws_ref[0], 0.0).astype(BF16)
    bs = bs_ref[0]
    lng = lng_ref[...]
    lnb = lnb_ref[...]
    for c in range(bm // ch):
        rows = slice(c * ch, (c + 1) * ch)
        vn = (v_ref[rows, :].astype(F32) - mu[rows]) * rstd[rows] * lng + lnb
        mixed = jnp.dot(ws, vn.astype(BF16), preferred_element_type=F32) + bs
        gate_ref[rows, :] = (u_ref[rows, :].astype(F32) * mixed).astype(BF16)
    part = jnp.dot(gate_ref[...], wo_ref[...], preferred_element_type=F32)

    @pl.when(g == 0)
    def _():
        acc_ref[...] = part

    @pl.when(g > 0)
    def _():
        acc_ref[...] += part

    @pl.when(g == ng - 1)
    def _():
        o_ref[...] = acc_ref[...] + bo_ref[...] + h_ref[...]


def sgu_out(u, v, st, ln_g, ln_b, w_s, b_s, w_out, b_out, h):
    m, e = u.shape
    d = w_out.shape[1]
    groups, ch, _ = w_s.shape
    gw = e // groups
    bm = _tile(m, SGU_BM)
    assert bm % ch == 0
    return pl.pallas_call(
        functools.partial(_sgu_out_kernel, width=e),
        out_shape=jax.ShapeDtypeStruct((m, d), F32),
        grid=(m // bm, groups),
        in_specs=[
            pl.BlockSpec((bm, gw), lambda i, g: (i, g)),
            pl.BlockSpec((bm, gw), lambda i, g: (i, g)),
            pl.BlockSpec((bm, V7X_LANES), lambda i, g: (i, 0)),
            pl.BlockSpec((1, gw), lambda i, g: (0, g)),
            pl.BlockSpec((1, gw), lambda i, g: (0, g)),
            pl.BlockSpec((1, ch, ch), lambda i, g: (g, 0, 0)),
            pl.BlockSpec((1, ch, 1), lambda i, g: (g, 0, 0)),
            pl.BlockSpec((gw, d), lambda i, g: (g, 0)),
            pl.BlockSpec((1, d), lambda i, g: (0, 0)),
            pl.BlockSpec((bm, d), lambda i, g: (i, 0)),
        ],
        out_specs=pl.BlockSpec((bm, d), lambda i, g: (i, 0)),
        scratch_shapes=[pltpu.VMEM((bm, d), F32), pltpu.VMEM((bm, gw), BF16)],
        compiler_params=_params("parallel", "arbitrary"),
        name="sgu_out",
    )(u, v, st, ln_g.reshape(1, e).astype(F32), ln_b.reshape(1, e).astype(F32),
      w_s.astype(F32), b_s.reshape(groups, ch, 1).astype(F32), w_out.astype(BF16),
      b_out.reshape(1, d).astype(F32), h)


def _rope_tables(seq, dim):
    half = dim // 2
    pos = jnp.arange(seq, dtype=F32)
    inv = ROPE_THETA ** (-(jnp.arange(0, dim, 2, dtype=F32) / dim))
    ang = pos[:, None] * inv[None, :]
    zeros = jnp.zeros((seq, V7X_LANES // 2 - half), F32)
    cos = jnp.concatenate([jnp.cos(ang), zeros, jnp.cos(ang), zeros], axis=1)
    sin = jnp.concatenate([-jnp.sin(ang), zeros, jnp.sin(ang), zeros], axis=1)
    return cos, sin


def _spread_rope_cols(w):
    k, dim = w.shape
    half = dim // 2
    zeros = jnp.zeros((k, V7X_LANES // 2 - half), w.dtype)
    return jnp.concatenate([w[:, :half], zeros, w[:, half:], zeros], axis=1)


def conformer_mixer(xn, h, w_in, b_in, dw_w, dw_b, ln_g, ln_b, w_out, b_out, *, bsz, seq):
    c = w_out.shape[0]
    wi = w_in.astype(BF16)
    z = matmul(xn, wi[:, :c], w2=wi[:, c:], bias=b_in[:c], bias2=b_in[c:])
    z = causal_dwconv(z.reshape(bsz, seq, c), dw_w, dw_b).reshape(bsz * seq, c)
    return matmul(z, w_out.astype(BF16), prologue="ln_silu", pg=ln_g, pb=ln_b, bias=b_out,
                  residual=h, out_dtype=F32)


def dilated_attention_mixer(xn, h, w_qkv, w_o, *, bsz, seq):
    d = xn.shape[1]
    dh = DSA_HEAD_DIM
    heads = w_o.shape[0] // dh
    groups = len(DSA_CONFIGS)
    hw = heads * dh
    w5 = w_qkv.reshape(d, groups, 3, hw)
    scale = jnp.array([dh ** -0.5, 1.0, 1.0], F32).reshape(1, 1, 3, 1)
    w5 = (w5 * scale).astype(BF16)
    tables = _rope_tables(seq, dh)
    outs, lses = [], []
    for g, (window, dil) in enumerate(DSA_CONFIGS):
        qkv = dsa_project(xn, w5[:, g].reshape(d, 3 * hw), tables, dil, bsz=bsz, seq=seq,
                          rope_cols=2 * hw)
        o, lse = band_attention(qkv, window, dil, bsz=bsz, seq=seq, heads=heads)
        outs.append(o)
        lses.append(lse)
    o = merge_groups(outs, lses, [dil for _, dil in DSA_CONFIGS], heads=heads)
    return matmul(o, w_o.astype(BF16), residual=h, out_dtype=F32)


def mla_mixer(xn, h, w_in, q_norm, w_qb, kv_norm, w_kvb, w_o, *, bsz, seq):
    q_rank = q_norm.shape[0]
    kv_rank = kv_norm.shape[0]
    assert q_rank == kv_rank
    heads = MLA_HEADS
    w_in_p = jnp.concatenate(
        [w_in[:, :q_rank + kv_rank], _spread_rope_cols(w_in[:, q_rank + kv_rank:])],
        axis=1).astype(BF16)
    n_c = w_in_p.shape[1]
    kpe_group = (q_rank + kv_rank) // V7X_LANES
    cmask = tuple(k == kpe_group for k in range(n_c // V7X_LANES))
    tables = _rope_tables(seq, MLA_ROPE)
    c = matmul(xn, w_in_p, rope=tables, rope_mask=cmask, seq=seq, out_dtype=F32, bn=n_c)
    q_scale = (MLA_NOPE + MLA_ROPE) ** -0.5 * math.log2(math.e)
    wq3 = w_qb.reshape(q_rank, heads, MLA_NOPE + MLA_ROPE) * q_scale
    wq_pe = jax.vmap(_spread_rope_cols, in_axes=1, out_axes=1)(wq3[:, :, MLA_NOPE:])
    wq_p = jnp.concatenate([wq3[:, :, :MLA_NOPE], wq_pe], axis=2)
    wq_p = wq_p.reshape(q_rank, heads * (MLA_NOPE + V7X_LANES)).astype(BF16)
    bn = _tile(wq_p.shape[1], MLA_PROJ_BN)
    qmask = tuple(k % 2 == 1 for k in range(bn // V7X_LANES))
    q = matmul(c, wq_p, x_cols=(0, q_rank), prologue="rms", pg=q_norm, rope=tables,
               rope_mask=qmask, seq=seq, bm=MM_BM, bn=bn)
    kv = matmul(c, w_kvb.astype(BF16), x_cols=(1, kv_rank), prologue="rms", pg=kv_norm,
                bm=MM_BM, bn=MLA_PROJ_BN)
    o = mla_attention(q, kv, c, bsz=bsz, seq=seq, kpe_block=kpe_group)
    return matmul(o, w_o.astype(BF16), residual=h, out_dtype=F32)


def sgu_mixer(xn, h, w_in, b_in, ln_g, ln_b, w_s, b_s, w_out, b_out):
    e = w_out.shape[0]
    wi = w_in.astype(BF16)
    u = matmul(xn, wi[:, :e], bias=b_in[:e], gelu=True)
    v, st = matmul(xn, wi[:, e:], bias=b_in[e:], gelu=True, stats=True)
    return sgu_out(u, v, st, ln_g, ln_b, w_s, b_s, w_out, b_out, h)


def kernel(x, l0_norm_mix, l0_cc_w_in, l0_cc_b_in, l0_cc_dw_w, l0_cc_dw_b, l0_cc_ln_g, l0_cc_ln_b, l0_cc_w_out, l0_cc_b_out, l0_norm_ffn, l0_ffn_w_up, l0_ffn_dw_w, l0_ffn_dw_b, l0_ffn_w_down, l1_norm_mix, l1_dsa_w_qkv, l1_dsa_w_o, l1_norm_ffn, l1_ffn_w_up, l1_ffn_dw_w, l1_ffn_dw_b, l1_ffn_w_down, l2_norm_mix, l2_mla_w_in, l2_mla_q_norm, l2_mla_w_qb, l2_mla_kv_norm, l2_mla_w_kvb, l2_mla_w_o, l2_norm_ffn, l2_ffn_w_up, l2_ffn_dw_w, l2_ffn_dw_b, l2_ffn_w_down, l3_norm_mix, l3_sg_w_in, l3_sg_b_in, l3_sg_ln_g, l3_sg_ln_b, l3_sg_w_s, l3_sg_b_s, l3_sg_w_out, l3_sg_b_out, l3_norm_ffn, l3_ffn_w_up, l3_ffn_dw_w, l3_ffn_dw_b, l3_ffn_w_down, final_norm):
    bsz, seq, d = x.shape
    dims = dict(bsz=bsz, seq=seq)
    h = x.reshape(bsz * seq, d)

    def ffn(h, norm, w_up, dw_w, dw_b, w_down, **tail):
        return conv_ffn(h, norm, w_up, dw_w, dw_b, w_down, seq=seq, **tail)

    feeds_next = dict(tail="residual_and_norm")
    h = conformer_mixer(rmsnorm(h, l0_norm_mix, BF16), h, l0_cc_w_in, l0_cc_b_in, l0_cc_dw_w,
                        l0_cc_dw_b, l0_cc_ln_g, l0_cc_ln_b, l0_cc_w_out, l0_cc_b_out, **dims)
    h, xn = ffn(h, l0_norm_ffn, l0_ffn_w_up, l0_ffn_dw_w, l0_ffn_dw_b, l0_ffn_w_down,
                tail_g=l1_norm_mix, **feeds_next)
    h = dilated_attention_mixer(xn, h, l1_dsa_w_qkv, l1_dsa_w_o, **dims)
    h, xn = ffn(h, l1_norm_ffn, l1_ffn_w_up, l1_ffn_dw_w, l1_ffn_dw_b, l1_ffn_w_down,
                tail_g=l2_norm_mix, **feeds_next)
    h = mla_mixer(xn, h, l2_mla_w_in, l2_mla_q_norm, l2_mla_w_qb, l2_mla_kv_norm, l2_mla_w_kvb,
                  l2_mla_w_o, **dims)
    h, xn = ffn(h, l2_norm_ffn, l2_ffn_w_up, l2_ffn_dw_w, l2_ffn_dw_b, l2_ffn_w_down,
                tail_g=l3_norm_mix, **feeds_next)
    h = sgu_mixer(xn, h, l3_sg_w_in, l3_sg_b_in, l3_sg_ln_g, l3_sg_ln_b, l3_sg_w_s, l3_sg_b_s,
                  l3_sg_w_out, l3_sg_b_out)
    out = ffn(h, l3_norm_ffn, l3_ffn_w_up, l3_ffn_dw_w, l3_ffn_dw_b, l3_ffn_w_down,
              tail="norm_only", tail_g=final_norm)
    return out.reshape(bsz, seq, d)
```
